```python
import math
import jax, jax.numpy as jnp
from jax import lax
import numpy as np

D_MODEL = 1024
BATCH = 8
SEQ = 4096
DEPTH = 4

SSM_WIDTH = 256
POOL_WIDTH = 256
ATTN_WIDTH = 512
MIX_WIDTH = SSM_WIDTH + POOL_WIDTH + ATTN_WIDTH
SSM_GROUP = 16
SSM_GROUPS = SSM_WIDTH // SSM_GROUP
SSM_STATE = 64
DT_MIN = 1e-3
DT_MAX = 1e-1
POOL_WINDOWS = (2, 4, 8, 16)
POOL_GROUP = POOL_WIDTH // len(POOL_WINDOWS)
HEAD_DIM = 64
N_HEADS = ATTN_WIDTH // HEAD_DIM
N_KV_HEADS = 2
KV_WIDTH = N_KV_HEADS * HEAD_DIM
Q_PER_KV = N_HEADS // N_KV_HEADS
WINDOW = 128
BLOCK = 128
RMS_EPS = 1e-6
IN_WIDTHS = (SSM_WIDTH, SSM_WIDTH, POOL_WIDTH, POOL_WIDTH, ATTN_WIDTH, KV_WIDTH, KV_WIDTH, ATTN_WIDTH)
IN_WIDTH = sum(IN_WIDTHS)
IN_SPLITS = [int(c) for c in np.cumsum(IN_WIDTHS)[:-1]]

kernel_name = "hybrid_s5_pool_swa_encoder"

F32 = jnp.float32


def rms_norm(x, g):
    xf = x.astype(F32)
    y = xf * lax.rsqrt(jnp.mean(xf * xf, axis=-1, keepdims=True) + RMS_EPS)
    return (y * g.astype(F32)).astype(x.dtype)


def _complex_linear_combine(left, right):
    a_re1, a_im1, b_re1, b_im1 = left
    a_re2, a_im2, b_re2, b_im2 = right
    a_re = a_re1 * a_re2 - a_im1 * a_im2
    a_im = a_re1 * a_im2 + a_im1 * a_re2
    b_re = a_re2 * b_re1 - a_im2 * b_im1 + b_re2
    b_im = a_re2 * b_im1 + a_im2 * b_re1 + b_im2
    return a_re, a_im, b_re, b_im


def ssm_direction(u, a_re, a_im, log_dt, b_re, b_im, c_re, c_im, reverse):
    a_re = a_re.astype(F32)
    a_im = a_im.astype(F32)
    dt = jnp.exp(log_dt.astype(F32))[:, None]
    mag = jnp.exp(a_re * dt)
    lb_re = mag * jnp.cos(a_im * dt)
    lb_im = mag * jnp.sin(a_im * dt)
    den = a_re * a_re + a_im * a_im
    num_re = lb_re - 1.0
    coef_re = (num_re * a_re + lb_im * a_im) / den
    coef_im = (lb_im * a_re - num_re * a_im) / den
    b_re = b_re.astype(F32)
    b_im = b_im.astype(F32)
    bb_re = coef_re[..., None] * b_re - coef_im[..., None] * b_im
    bb_im = coef_re[..., None] * b_im + coef_im[..., None] * b_re
    bu_re = jnp.einsum('blgp,gnp->blgn', u, bb_re)
    bu_im = jnp.einsum('blgp,gnp->blgn', u, bb_im)
    lam_re = jnp.broadcast_to(lb_re, bu_re.shape)
    lam_im = jnp.broadcast_to(lb_im, bu_im.shape)
    _, _, h_re, h_im = lax.associative_scan(
        _complex_linear_combine, (lam_re, lam_im, bu_re, bu_im), reverse=reverse, axis=1)
    return (jnp.einsum('blgn,gpn->blgp', h_re, c_re.astype(F32))
            - jnp.einsum('blgn,gpn->blgp', h_im, c_im.astype(F32)))


def ssm_branch(u, a_re, a_im, log_dt, b_re, b_im, c_re, c_im, d, glu_w, glu_b):
    bsz, seq, _ = u.shape
    uf = u.astype(F32)
    ug = uf.reshape(bsz, seq, SSM_GROUPS, SSM_GROUP)
    y = (ssm_direction(ug, a_re[0], a_im[0], log_dt[0], b_re[0], b_im[0], c_re[0], c_im[0], False)
         + ssm_direction(ug, a_re[1], a_im[1], log_dt[1], b_re[1], b_im[1], c_re[1], c_im[1], True))
    y = y.reshape(bsz, seq, SSM_WIDTH) + d.astype(F32) * uf
    y = jax.nn.gelu(y)
    z = y @ glu_w.astype(F32) + glu_b.astype(F32)
    za, zb = jnp.split(z, 2, axis=-1)
    return (za * jax.nn.sigmoid(zb)).astype(u.dtype)


def pool_branch(p, pool_w, pool_scale):
    bsz, seq, _ = p.shape
    pf = p.astype(F32)
    cs = jnp.pad(jnp.cumsum(pf, axis=1), ((0, 0), (1, 0), (0, 0)))
    pos = jnp.arange(seq)
    outs = []
    for gi, w in enumerate(POOL_WINDOWS):
        lo = jnp.clip(pos - w // 2, 0, seq)
        hi = jnp.clip(pos + w // 2, 0, seq)
        cs_g = cs[:, :, gi * POOL_GROUP:(gi + 1) * POOL_GROUP]
        mean = (cs_g[:, hi] - cs_g[:, lo]) / (hi - lo).astype(F32)[None, :, None]
        outs.append(mean - pf[:, :, gi * POOL_GROUP:(gi + 1) * POOL_GROUP])
    mixed = jnp.stack(outs, axis=2)
    y = jnp.einsum('blgc,gcd->blgd', mixed, pool_w.astype(F32)).reshape(bsz, seq, POOL_WIDTH)
    return (y * pool_scale.astype(F32)).astype(p.dtype)


def alibi_slopes():
    return jnp.exp2(-8.0 * jnp.arange(1, N_HEADS + 1, dtype=F32) / N_HEADS)


def window_attention(q, k, v, sink):
    bsz, seq, _ = q.shape
    nb = seq // BLOCK
    qf = q.astype(F32).reshape(bsz, nb, BLOCK, N_KV_HEADS, Q_PER_KV, HEAD_DIM) * (HEAD_DIM ** -0.5)

    def band(t):
        t = t.astype(F32).reshape(bsz, nb, BLOCK, N_KV_HEADS, HEAD_DIM)
        tp = jnp.pad(t, ((0, 0), (1, 1), (0, 0), (0, 0), (0, 0)))
        return jnp.concatenate([tp[:, :-2], tp[:, 1:-1], tp[:, 2:]], axis=2)

    kb = band(k)
    vb = band(v)
    scores = jnp.einsum('bnqkgd,bnskd->bnkgqs', qf, kb)
    blk = jnp.arange(nb)[:, None]
    qpos = blk * BLOCK + jnp.arange(BLOCK)[None, :]
    kpos = (blk - 1) * BLOCK + jnp.arange(3 * BLOCK)[None, :]
    dist = jnp.abs(qpos[:, :, None] - kpos[:, None, :])
    valid = (dist <= WINDOW) & (kpos[:, None, :] >= 0) & (kpos[:, None, :] < seq)
    slopes = alibi_slopes().reshape(N_KV_HEADS, Q_PER_KV)
    bias = -slopes[None, :, :, None, None] * dist.astype(F32)[:, None, None]
    scores = jnp.where(valid[None, :, None, None], scores + bias[None], -jnp.inf)
    sink_b = sink.astype(F32).reshape(1, 1, N_KV_HEADS, Q_PER_KV, 1, 1)
    m = jnp.maximum(jnp.max(scores, axis=-1, keepdims=True), sink_b)
    pr = jnp.exp(scores - m)
    pr = pr / (jnp.sum(pr, axis=-1, keepdims=True) + jnp.exp(sink_b - m))
    out = jnp.einsum('bnkgqs,bnskd->bnqkgd', pr, vb)
    return out.reshape(bsz, seq, ATTN_WIDTH).astype(q.dtype)


def hybrid_layer(x, pre_g, w_in, a_re, a_im, log_dt, b_re, b_im, c_re, c_im, d, glu_w, glu_b,
                 pool_w, pool_scale, sink, w_out, post_g):
    h = rms_norm(x, pre_g)
    proj = h @ w_in
    su, sg, pu, pg, q, k, v, ag = jnp.split(proj, IN_SPLITS, axis=-1)
    y_ssm = ssm_branch(su, a_re, a_im, log_dt, b_re, b_im, c_re, c_im, d, glu_w, glu_b) * jax.nn.silu(sg)
    y_pool = pool_branch(pu, pool_w, pool_scale) * jax.nn.silu(pg)
    y_attn = window_attention(q, k, v, sink) * jax.nn.silu(ag)
    y = jnp.concatenate([y_ssm, y_pool, y_attn], axis=-1) @ w_out
    return x + rms_norm(y, post_g)


def setup_inputs(seed: int = 0) -> dict:
    key = jax.random.key(seed)
    ks = jax.random.split(key, 20)
    nrm = lambda k, shape, s: jax.random.normal(k, shape, F32) * s
    n_idx = jnp.arange(SSM_STATE, dtype=F32)
    a_re = -0.5 + nrm(ks[3], (DEPTH, 2, SSM_GROUPS, SSM_STATE), 0.01)
    a_im = math.pi * n_idx + nrm(ks[4], (DEPTH, 2, SSM_GROUPS, SSM_STATE), 0.01)
    log_dt = jax.random.uniform(ks[5], (DEPTH, 2, SSM_GROUPS), F32,
                                math.log(DT_MIN), math.log(DT_MAX))
    return {
        "x": nrm(ks[0], (BATCH, SEQ, D_MODEL), 1.0),
        "pre_norm_g": 1.0 + nrm(ks[1], (DEPTH, D_MODEL), 0.05),
        "w_in": nrm(ks[2], (DEPTH, D_MODEL, IN_WIDTH), D_MODEL ** -0.5),
        "ssm_a_re": a_re,
        "ssm_a_im": a_im,
        "ssm_log_dt": log_dt,
        "ssm_b_re": nrm(ks[6], (DEPTH, 2, SSM_GROUPS, SSM_STATE, SSM_GROUP), (2 * SSM_GROUP) ** -0.5),
        "ssm_b_im": nrm(ks[7], (DEPTH, 2, SSM_GROUPS, SSM_STATE, SSM_GROUP), (2 * SSM_GROUP) ** -0.5),
        "ssm_c_re": nrm(ks[8], (DEPTH, 2, SSM_GROUPS, SSM_GROUP, SSM_STATE), SSM_STATE ** -0.5),
        "ssm_c_im": nrm(ks[9], (DEPTH, 2, SSM_GROUPS, SSM_GROUP, SSM_STATE), SSM_STATE ** -0.5),
        "ssm_d": nrm(ks[10], (DEPTH, SSM_WIDTH), 0.5),
        "ssm_glu_w": nrm(ks[11], (DEPTH, SSM_WIDTH, 2 * SSM_WIDTH), SSM_WIDTH ** -0.5),
        "ssm_glu_b": nrm(ks[12], (DEPTH, 2 * SSM_WIDTH), 0.01),
        "pool_w": nrm(ks[13], (DEPTH, len(POOL_WINDOWS), POOL_GROUP, POOL_GROUP), POOL_GROUP ** -0.5),
        "pool_scale": 0.5 + nrm(ks[14], (DEPTH, POOL_WIDTH), 0.1),
        "attn_sink": nrm(ks[15], (DEPTH, N_HEADS), 0.5),
        "w_out": nrm(ks[16], (DEPTH, MIX_WIDTH, D_MODEL), MIX_WIDTH ** -0.5),
        "post_norm_g": 1.0 + nrm(ks[17], (DEPTH, D_MODEL), 0.05),
    }


def reference(x, pre_norm_g, w_in, ssm_a_re, ssm_a_im, ssm_log_dt, ssm_b_re, ssm_b_im, ssm_c_re,
              ssm_c_im, ssm_d, ssm_glu_w, ssm_glu_b, pool_w, pool_scale, attn_sink, w_out, post_norm_g):
    for l in range(DEPTH):
        x = hybrid_layer(x, pre_norm_g[l], w_in[l], ssm_a_re[l], ssm_a_im[l], ssm_log_dt[l],
                         ssm_b_re[l], ssm_b_im[l], ssm_c_re[l], ssm_c_im[l], ssm_d[l],
                         ssm_glu_w[l], ssm_glu_b[l], pool_w[l], pool_scale[l], attn_sink[l],
                         w_out[l], post_norm_g[l])
    return x
```

```python
import functools
import math

import numpy as np
import jax
import jax.numpy as jnp
from jax import lax
from jax.experimental import pallas as pl
from jax.experimental.pallas import tpu as pltpu

F32 = jnp.float32
BF16 = jnp.bfloat16
HIGHEST = lax.Precision.HIGHEST

D_MODEL = 1024
SSM_WIDTH = 256
POOL_WIDTH = 256
ATTN_WIDTH = 512
SSM_GROUP = 16
SSM_GROUPS = SSM_WIDTH // SSM_GROUP
SSM_PAIRS = SSM_GROUPS // 2
SSM_STATE = 64
POOL_WINDOWS = (2, 4, 8, 16)
POOL_GROUP = POOL_WIDTH // len(POOL_WINDOWS)
POOL_HALO = 8
HEAD_DIM = 64
N_HEADS = ATTN_WIDTH // HEAD_DIM
N_KV_HEADS = 2
KV_WIDTH = N_KV_HEADS * HEAD_DIM
Q_PER_KV = N_HEADS // N_KV_HEADS
WINDOW = 128
BLOCK = 128
RMS_EPS = 1e-6
NEG_BIG = -1e30

SSM_CHUNK = 16
CHUNK_WIDTH = SSM_CHUNK * SSM_GROUP
PAIR_WIDTH = 2 * CHUNK_WIDTH
SUBLANES = 8

COL_SU, COL_SG, COL_PU, COL_PG = 0, 256, 512, 768
COL_Q, COL_AG, COL_K, COL_V = 1024, 1536, 2048, 2176
IN_WIDTH = COL_V + KV_WIDTH


def _in_col_permutation():
    idx = np.arange(IN_WIDTH)
    ref_q, ref_k, ref_v, ref_ag = 1024, 1536, 1664, 1792
    return np.concatenate([idx[:ref_k], idx[ref_ag:ref_ag + ATTN_WIDTH],
                           idx[ref_k:ref_k + KV_WIDTH], idx[ref_v:ref_v + KV_WIDTH]])


def _inproj_kernel(x_ref, g_ref, w_ref, o_ref, h_ref):
    x = x_ref[...]
    ms = jnp.mean(x * x, axis=-1, keepdims=True)
    h_ref[...] = (x * lax.rsqrt(ms + RMS_EPS) * g_ref[...]).astype(BF16)
    n_tile = 256
    for j in range(IN_WIDTH // n_tile):
        sl = slice(j * n_tile, (j + 1) * n_tile)
        o_ref[:, sl] = jnp.dot(h_ref[...], w_ref[:, sl], preferred_element_type=F32).astype(BF16)


def _inproj(x2d, g, w, tm=512):
    n_tok = x2d.shape[0]
    return pl.pallas_call(
        _inproj_kernel,
        grid=(n_tok // tm,),
        in_specs=[pl.BlockSpec((tm, D_MODEL), lambda i: (i, 0)),
                  pl.BlockSpec((1, D_MODEL), lambda i: (0, 0)),
                  pl.BlockSpec((D_MODEL, IN_WIDTH), lambda i: (0, 0))],
        out_specs=pl.BlockSpec((tm, IN_WIDTH), lambda i: (i, 0)),
        out_shape=jax.ShapeDtypeStruct((n_tok, IN_WIDTH), BF16),
        scratch_shapes=[pltpu.VMEM((tm, D_MODEL), BF16)],
        compiler_params=pltpu.CompilerParams(dimension_semantics=("arbitrary",),
                                             vmem_limit_bytes=48 * 1024 * 1024),
        name="inproj",
    )(x2d, g, w)


def _ssm_weights(a_re, a_im, log_dt, b_re, b_im, c_re, c_im):
    depth = a_re.shape[0]
    T = SSM_CHUNK
    dt = jnp.exp(log_dt)[..., None]
    ar, ai = a_re * dt, a_im * dt
    mag = jnp.exp(ar)
    lb_re, lb_im = mag * jnp.cos(ai), mag * jnp.sin(ai)
    den = a_re * a_re + a_im * a_im
    num_re = lb_re - 1.0
    coef_re = (num_re * a_re + lb_im * a_im) / den
    coef_im = (lb_im * a_re - num_re * a_im) / den
    bb_re = coef_re[..., None] * b_re - coef_im[..., None] * b_im
    bb_im = coef_re[..., None] * b_im + coef_im[..., None] * b_re
    tau = jnp.arange(T + 1, dtype=F32)[:, None, None]
    pw_mag = jnp.exp(tau * ar[:, :, None])
    pw_re = pw_mag * jnp.cos(tau * ai[:, :, None])
    pw_im = pw_mag * jnp.sin(tau * ai[:, :, None])
    m_re = c_re[:, :, None] * pw_re[..., None, :] - c_im[:, :, None] * pw_im[..., None, :]
    m_im = c_re[:, :, None] * pw_im[..., None, :] + c_im[:, :, None] * pw_re[..., None, :]
    kern = (jnp.einsum('dstgqn,dsgnp->dstgqp', m_re, bb_re, precision=HIGHEST)
            - jnp.einsum('dstgqn,dsgnp->dstgqp', m_im, bb_im, precision=HIGHEST))

    ii = jnp.arange(T)[:, None]
    jj = jnp.arange(T)[None, :]
    lag_f = jnp.clip(jj - ii, 0, T)
    lag_b = jnp.clip(ii - jj, 0, T)
    tf = kern[:, 0][:, lag_f] * (jj >= ii).astype(F32)[None, :, :, None, None, None]
    tb = kern[:, 1][:, lag_b] * (ii >= jj).astype(F32)[None, :, :, None, None, None]
    w_t = jnp.transpose(tf + tb, (0, 3, 1, 5, 2, 4)).reshape(depth, SSM_GROUPS, CHUNK_WIDTH, CHUNK_WIDTH)

    eye2 = jnp.eye(2, dtype=F32)

    def pair_cols(a):
        return a.reshape((depth, SSM_PAIRS, 2) + a.shape[2:])

    def state_in(direction, expo):
        p_re = pw_re[:, direction][:, expo]
        p_im = pw_im[:, direction][:, expo]
        br = jnp.transpose(bb_re[:, direction], (0, 1, 3, 2))
        bi = jnp.transpose(bb_im[:, direction], (0, 1, 3, 2))
        a_r = (jnp.transpose(p_re, (0, 2, 1, 3))[:, :, :, None] * br[:, :, None]
               - jnp.transpose(p_im, (0, 2, 1, 3))[:, :, :, None] * bi[:, :, None])
        a_i = (jnp.transpose(p_re, (0, 2, 1, 3))[:, :, :, None] * bi[:, :, None]
               + jnp.transpose(p_im, (0, 2, 1, 3))[:, :, :, None] * br[:, :, None])
        a = jnp.stack([pair_cols(a_r), pair_cols(a_i)], axis=5)
        w = jnp.einsum('dmeipcn,ef->dmeipcfn', a, eye2)
        return w.reshape(depth, SSM_PAIRS, PAIR_WIDTH, 4 * SSM_STATE)

    w_sf = state_in(0, jnp.arange(T - 1, -1, -1))
    w_sb = state_in(1, jnp.arange(T))

    def state_out(direction, expo):
        mr = jnp.transpose(m_re[:, direction][:, expo], (0, 2, 4, 1, 3))
        mi = jnp.transpose(m_im[:, direction][:, expo], (0, 2, 4, 1, 3))
        a = jnp.stack([pair_cols(mr), pair_cols(-mi)], axis=2)
        w = jnp.einsum('dmcenjq,ef->dmcenfjq', a, eye2)
        return w.reshape(depth, SSM_PAIRS, 4 * SSM_STATE, PAIR_WIDTH)

    w_hf = state_out(0, jnp.arange(1, T + 1))
    w_hb = state_out(1, jnp.arange(T, 0, -1))

    def lam_pairs(p):
        p = p.reshape(depth, SSM_PAIRS, 1, 2 * SSM_STATE)
        return jnp.broadcast_to(p, (depth, SSM_PAIRS, SUBLANES, 2 * SSM_STATE))

    lam = jnp.stack([lam_pairs(pw_re[:, 0, T]), lam_pairs(pw_im[:, 0, T]),
                     lam_pairs(pw_re[:, 1, T]), lam_pairs(pw_im[:, 1, T])], axis=1)
    return (w_t.astype(BF16), w_sf.astype(BF16), w_sb.astype(BF16),
            w_hf.astype(BF16), w_hb.astype(BF16), lam)


def _ssm_scan_kernel(zf_ref, zb_ref, wsf_ref, wsb_ref, lam_ref, hf_out, hb_out,
                     sfr, sfi, sbr, sbi, carry):
    rc = zf_ref.shape[1]
    half = 2 * SSM_STATE

    @pl.when(pl.program_id(0) == 0)
    def _():
        carry[...] = jnp.zeros_like(carry)

    for m in range(SSM_PAIRS):
        s = jnp.dot(zf_ref[m], wsf_ref[m], preferred_element_type=F32)
        sfr[m] = s[:, :half]
        sfi[m] = s[:, half:]
        s = jnp.dot(zb_ref[m], wsb_ref[m], preferred_element_type=F32)
        sbr[m] = s[:, :half]
        sbi[m] = s[:, half:]

    lfr, lfi, lbr, lbi = lam_ref[0], lam_ref[1], lam_ref[2], lam_ref[3]
    n_steps = rc // SUBLANES

    def body(k, c):
        hfr, hfi, hbr, hbi = c
        r = pl.multiple_of(k * SUBLANES, SUBLANES)
        hf_out[:, pl.ds(r, SUBLANES), :half] = hfr
        hf_out[:, pl.ds(r, SUBLANES), half:] = hfi
        nfr = lfr * hfr - lfi * hfi + sfr[:, pl.ds(r, SUBLANES), :]
        nfi = lfr * hfi + lfi * hfr + sfi[:, pl.ds(r, SUBLANES), :]
        rb = pl.multiple_of(rc - SUBLANES - k * SUBLANES, SUBLANES)
        hb_out[:, pl.ds(rb, SUBLANES), :half] = hbr
        hb_out[:, pl.ds(rb, SUBLANES), half:] = hbi
        nbr = lbr * hbr - lbi * hbi + sbr[:, pl.ds(rb, SUBLANES), :]
        nbi = lbr * hbi + lbi * hbr + sbi[:, pl.ds(rb, SUBLANES), :]
        return nfr, nfi, nbr, nbi

    c = lax.fori_loop(0, n_steps, body, (carry[0], carry[1], carry[2], carry[3]))
    carry[0], carry[1], carry[2], carry[3] = c


def _ssm_scan(zp, w_sf, w_sb, lam, rc=256):
    n_rows = zp.shape[1]
    n_tiles = n_rows // rc
    half = 2 * SSM_STATE
    out = jax.ShapeDtypeStruct((SSM_PAIRS, n_rows, 2 * half), F32)
    s_scratch = pltpu.VMEM((SSM_PAIRS, rc, half), F32)
    return pl.pallas_call(
        _ssm_scan_kernel,
        grid=(n_tiles,),
        in_specs=[pl.BlockSpec((SSM_PAIRS, rc, PAIR_WIDTH), lambda i: (0, i, 0)),
                  pl.BlockSpec((SSM_PAIRS, rc, PAIR_WIDTH), lambda i: (0, n_tiles - 1 - i, 0)),
                  pl.BlockSpec((SSM_PAIRS, PAIR_WIDTH, 2 * half), lambda i: (0, 0, 0)),
                  pl.BlockSpec((SSM_PAIRS, PAIR_WIDTH, 2 * half), lambda i: (0, 0, 0)),
                  pl.BlockSpec((4, SSM_PAIRS, SUBLANES, half), lambda i: (0, 0, 0, 0))],
        out_specs=[pl.BlockSpec((SSM_PAIRS, rc, 2 * half), lambda i: (0, i, 0)),
                   pl.BlockSpec((SSM_PAIRS, rc, 2 * half), lambda i: (0, n_tiles - 1 - i, 0))],
        out_shape=[out, out],
        scratch_shapes=[s_scratch, s_scratch, s_scratch, s_scratch,
                        pltpu.VMEM((4, SSM_PAIRS, SUBLANES, half), F32)],
        compiler_params=pltpu.CompilerParams(dimension_semantics=("arbitrary",),
                                             vmem_limit_bytes=48 * 1024 * 1024),
        name="ssm_scan",
    )(zp, zp, w_sf, w_sb, lam)


def _ssm_out_kernel(z_ref, hf_ref, hb_ref, wt_ref, whf_ref, whb_ref, y_ref):
    z = z_ref[...]
    y = jnp.dot(hf_ref[...].astype(BF16), whf_ref[...], preferred_element_type=F32)
    y = y + jnp.dot(hb_ref[...].astype(BF16), whb_ref[...], preferred_element_type=F32)
    y0 = jnp.dot(z[:, :CHUNK_WIDTH], wt_ref[0], preferred_element_type=F32)
    y1 = jnp.dot(z[:, CHUNK_WIDTH:], wt_ref[1], preferred_element_type=F32)
    y_ref[:, :CHUNK_WIDTH] = (y[:, :CHUNK_WIDTH] + y0).astype(BF16)
    y_ref[:, CHUNK_WIDTH:] = (y[:, CHUNK_WIDTH:] + y1).astype(BF16)


def _ssm_out(zp, hf, hb, w_t, w_hf, w_hb, rt=512):
    n_rows = zp.shape[1]
    half2 = 4 * SSM_STATE
    return pl.pallas_call(
        _ssm_out_kernel,
        grid=(SSM_PAIRS, n_rows // rt),
        in_specs=[pl.BlockSpec((None, rt, PAIR_WIDTH), lambda m, i: (m, i, 0)),
                  pl.BlockSpec((None, rt, half2), lambda m, i: (m, i, 0)),
                  pl.BlockSpec((None, rt, half2), lambda m, i: (m, i, 0)),
                  pl.BlockSpec((2, CHUNK_WIDTH, CHUNK_WIDTH), lambda m, i: (m, 0, 0)),
                  pl.BlockSpec((None, half2, PAIR_WIDTH), lambda m, i: (m, 0, 0)),
                  pl.BlockSpec((None, half2, PAIR_WIDTH), lambda m, i: (m, 0, 0))],
        out_specs=pl.BlockSpec((None, rt, PAIR_WIDTH), lambda m, i: (m, i, 0)),
        out_shape=jax.ShapeDtypeStruct((SSM_PAIRS, n_rows, PAIR_WIDTH), BF16),
        compiler_params=pltpu.CompilerParams(dimension_semantics=("arbitrary", "arbitrary")),
        name="ssm_out",
    )(zp, hf, hb, w_t, w_hf, w_hb)


def _to_chunk_layout(proj, bsz, seq):
    n_chunks = seq // SSM_CHUNK
    su = proj[:, COL_SU:COL_SU + SSM_WIDTH].reshape(bsz, n_chunks, SSM_CHUNK, SSM_PAIRS, 2, SSM_GROUP)
    z = jnp.transpose(su, (3, 1, 0, 4, 2, 5))
    return z.reshape(SSM_PAIRS, n_chunks * bsz, PAIR_WIDTH)


def _from_chunk_layout(yp, bsz, seq):
    n_chunks = seq // SSM_CHUNK
    y = yp.reshape(SSM_PAIRS, n_chunks, bsz, 2, SSM_CHUNK, SSM_GROUP)
    y = jnp.transpose(y, (2, 1, 4, 0, 3, 5))
    return y.reshape(bsz * seq, SSM_WIDTH)


def _pool_kernel(pu_ref, pg_ref, w_ref, scale_ref, o_ref, pad_ref, *, seq, rows):
    halo = POOL_HALO
    zeros = jnp.zeros((halo, POOL_WIDTH), F32)
    pad_ref[0:halo, :] = zeros
    pad_ref[seq + halo:seq + 2 * halo, :] = zeros
    pad_ref[halo:seq + halo, :] = pu_ref[...].astype(F32)

    lane = lax.broadcasted_iota(jnp.int32, (rows, POOL_WIDTH), 1)
    half_win = jnp.where(lane < POOL_GROUP, 1,
                         jnp.where(lane < 2 * POOL_GROUP, 2,
                                   jnp.where(lane < 3 * POOL_GROUP, 4, 8)))
    row = lax.broadcasted_iota(jnp.int32, (rows, POOL_WIDTH), 0)

    ext = rows + 2 * halo

    def shift(a, d):
        return pltpu.roll(a, (-d) % ext, axis=0)

    def body(k, _):
        base = pl.multiple_of(k * rows, rows)
        xs = pad_ref[pl.ds(base, ext), :]
        w2 = shift(xs, -1) + xs
        w4 = shift(w2, -1) + shift(w2, 1)
        w8 = shift(w4, -2) + shift(w4, 2)
        w16 = shift(w8, -4) + shift(w8, 4)
        centre = xs[halo:halo + rows]
        total = jnp.where(half_win == 1, w2[halo:halo + rows],
                          jnp.where(half_win == 2, w4[halo:halo + rows],
                                    jnp.where(half_win == 4, w8[halo:halo + rows],
                                              w16[halo:halo + rows])))
        t = row + base
        count = jnp.minimum(t + half_win, seq) - jnp.maximum(t - half_win, 0)
        diff = total / count.astype(F32) - centre
        y = jnp.dot(diff.astype(BF16), w_ref[...], preferred_element_type=F32)
        g = pg_ref[pl.ds(base, rows), :].astype(F32)
        o_ref[pl.ds(base, rows), :] = (y * scale_ref[...] * (g * jax.nn.sigmoid(g))).astype(BF16)
        return 0

    lax.fori_loop(0, seq // rows, body, 0)


def _pool(proj, w_bd, scale, bsz, seq, rows=256):
    kern = functools.partial(_pool_kernel, seq=seq, rows=rows)
    return pl.pallas_call(
        kern,
        grid=(bsz,),
        in_specs=[pl.BlockSpec((seq, POOL_WIDTH), lambda b: (b, COL_PU // POOL_WIDTH)),
                  pl.BlockSpec((seq, POOL_WIDTH), lambda b: (b, COL_PG // POOL_WIDTH)),
                  pl.BlockSpec((POOL_WIDTH, POOL_WIDTH), lambda b: (0, 0)),
                  pl.BlockSpec((1, POOL_WIDTH), lambda b: (0, 0))],
        out_specs=pl.BlockSpec((seq, POOL_WIDTH), lambda b: (b, 0)),
        out_shape=jax.ShapeDtypeStruct((bsz * seq, POOL_WIDTH), BF16),
        scratch_shapes=[pltpu.VMEM((seq + 2 * POOL_HALO, POOL_WIDTH), F32)],
        compiler_params=pltpu.CompilerParams(dimension_semantics=("arbitrary",)),
        name="pool",
    )(proj, proj, w_bd, scale)


def _attn_bias():
    slopes = np.exp2(-8.0 * np.arange(1, N_HEADS + 1, dtype=np.float32) / N_HEADS).astype(np.float32)
    qpos = np.arange(BLOCK)[:, None]
    kpos = np.arange(3 * BLOCK)[None, :] - BLOCK
    dist = np.abs(qpos - kpos)
    bias = -slopes[:, None, None] * dist.astype(np.float32)[None]
    return np.where(dist[None] <= WINDOW, bias, np.float32(NEG_BIG)).astype(np.float32)


def _attn_kernel(sink_ref, q_ref, kp_ref, kc_ref, kn_ref, vp_ref, vc_ref, vn_ref, ag_ref, bias_ref,
                 o_ref, *, n_blocks, seq):
    n = pl.program_id(1)
    kk = jnp.concatenate([kp_ref[...], kc_ref[...], kn_ref[...]], axis=0)
    vv = jnp.concatenate([vp_ref[...], vc_ref[...], vn_ref[...]], axis=0)
    kpos = lax.broadcasted_iota(jnp.int32, (1, 3 * BLOCK), 1) + (n - 1) * BLOCK
    edge = jnp.where((kpos >= 0) & (kpos < seq), 0.0, NEG_BIG).astype(F32)
    scale = HEAD_DIM ** -0.5
    for h in range(N_HEADS):
        kv = h // Q_PER_KV
        qh = q_ref[:, h * HEAD_DIM:(h + 1) * HEAD_DIM]
        kh = kk[:, kv * HEAD_DIM:(kv + 1) * HEAD_DIM]
        vh = vv[:, kv * HEAD_DIM:(kv + 1) * HEAD_DIM]
        s = lax.dot_general(qh, kh, (((1,), (1,)), ((), ())), preferred_element_type=F32)
        s = s * scale + bias_ref[h] + edge
        sink = sink_ref[h]
        m = jnp.maximum(jnp.max(s, axis=-1, keepdims=True), sink)
        p = jnp.exp(s - m)
        denom = jnp.sum(p, axis=-1, keepdims=True) + jnp.exp(sink - m)
        o = jnp.dot(p.astype(BF16), vh, preferred_element_type=F32) / denom
        g = ag_ref[:, h * HEAD_DIM:(h + 1) * HEAD_DIM].astype(F32)
        o_ref[:, h * HEAD_DIM:(h + 1) * HEAD_DIM] = (o * (g * jax.nn.sigmoid(g))).astype(BF16)


def _attention(proj, sink, bias, bsz, seq):
    nb = seq // BLOCK
    kern = functools.partial(_attn_kernel, n_blocks=nb, seq=seq)
    kcol, vcol = COL_K // KV_WIDTH, COL_V // KV_WIDTH

    def kv_spec(col, off):
        return pl.BlockSpec((BLOCK, KV_WIDTH),
                            lambda b, n: (b * nb + jnp.clip(n + off, 0, nb - 1), col))

    return pl.pallas_call(
        kern,
        grid=(bsz, nb),
        in_specs=[pl.BlockSpec(memory_space=pltpu.SMEM),
                  pl.BlockSpec((BLOCK, ATTN_WIDTH), lambda b, n: (b * nb + n, COL_Q // ATTN_WIDTH)),
                  kv_spec(kcol, -1), kv_spec(kcol, 0), kv_spec(kcol, 1),
                  kv_spec(vcol, -1), kv_spec(vcol, 0), kv_spec(vcol, 1),
                  pl.BlockSpec((BLOCK, ATTN_WIDTH), lambda b, n: (b * nb + n, COL_AG // ATTN_WIDTH)),
                  pl.BlockSpec((N_HEADS, BLOCK, 3 * BLOCK), lambda b, n: (0, 0, 0))],
        out_specs=pl.BlockSpec((BLOCK, ATTN_WIDTH), lambda b, n: (b * nb + n, 0)),
        out_shape=jax.ShapeDtypeStruct((bsz * seq, ATTN_WIDTH), BF16),
        compiler_params=pltpu.CompilerParams(dimension_semantics=("arbitrary", "arbitrary")),
        name="attention",
    )(sink, proj, proj, proj, proj, proj, proj, proj, proj, bias)


def _out_kernel(x_ref, ypre_ref, su_ref, sg_ref, ypool_ref, yattn_ref, d_ref, gw_ref, gb_ref,
                wo_ref, pg_ref, o_ref):
    u = su_ref[...].astype(F32)
    y = jax.nn.gelu(ypre_ref[...].astype(F32) + d_ref[...] * u)
    z = jnp.dot(y.astype(BF16), gw_ref[...], preferred_element_type=F32) + gb_ref[...]
    sg = sg_ref[...].astype(F32)
    y_ssm = z[:, :SSM_WIDTH] * jax.nn.sigmoid(z[:, SSM_WIDTH:]) * (sg * jax.nn.sigmoid(sg))
    acc = jnp.dot(y_ssm.astype(BF16), wo_ref[0:SSM_WIDTH, :], preferred_element_type=F32)
    acc = acc + jnp.dot(ypool_ref[...], wo_ref[SSM_WIDTH:SSM_WIDTH + POOL_WIDTH, :],
                        preferred_element_type=F32)
    acc = acc + jnp.dot(yattn_ref[...], wo_ref[SSM_WIDTH + POOL_WIDTH:, :], preferred_element_type=F32)
    ms = jnp.mean(acc * acc, axis=-1, keepdims=True)
    o_ref[...] = x_ref[...] + acc * lax.rsqrt(ms + RMS_EPS) * pg_ref[...]


def _out_proj(x2d, ypre, proj, ypool, yattn, d, glu_w, glu_b, w_out, post_g, tm=512):
    n_tok = x2d.shape[0]
    row = lambda c: (lambda i: (i, c))
    const = lambda i: (0, 0)
    return pl.pallas_call(
        _out_kernel,
        grid=(n_tok // tm,),
        in_specs=[pl.BlockSpec((tm, D_MODEL), row(0)),
                  pl.BlockSpec((tm, SSM_WIDTH), row(0)),
                  pl.BlockSpec((tm, SSM_WIDTH), row(COL_SU // SSM_WIDTH)),
                  pl.BlockSpec((tm, SSM_WIDTH), row(COL_SG // SSM_WIDTH)),
                  pl.BlockSpec((tm, POOL_WIDTH), row(0)),
                  pl.BlockSpec((tm, ATTN_WIDTH), row(0)),
                  pl.BlockSpec((1, SSM_WIDTH), const),
                  pl.BlockSpec((SSM_WIDTH, 2 * SSM_WIDTH), const),
                  pl.BlockSpec((1, 2 * SSM_WIDTH), const),
                  pl.BlockSpec((D_MODEL, D_MODEL), const),
                  pl.BlockSpec((1, D_MODEL), const)],
        out_specs=pl.BlockSpec((tm, D_MODEL), row(0)),
        out_shape=jax.ShapeDtypeStruct((n_tok, D_MODEL), F32),
        compiler_params=pltpu.CompilerParams(dimension_semantics=("arbitrary",),
                                             vmem_limit_bytes=48 * 1024 * 1024),
        name="out_proj",
    )(x2d, ypre, proj, proj, ypool, yattn, d, glu_w, glu_b, w_out, post_g)


def kernel(x, pre_norm_g, w_in, ssm_a_re, ssm_a_im, ssm_log_dt, ssm_b_re, ssm_b_im, ssm_c_re, ssm_c_im,
           ssm_d, ssm_glu_w, ssm_glu_b, pool_w, pool_scale, attn_sink, w_out, post_norm_g):
    bsz, seq, _ = x.shape
    depth = w_in.shape[0]
    assert seq % BLOCK == 0 and seq % SSM_CHUNK == 0 and bsz == SUBLANES

    w_in_b = w_in[:, :, _in_col_permutation()].astype(BF16)
    w_out_b = w_out.astype(BF16)
    glu_w_b = ssm_glu_w.astype(BF16)
    eye = jnp.eye(len(POOL_WINDOWS), dtype=F32)
    pool_bd = jnp.einsum('lgcd,gh->lgchd', pool_w.astype(F32), eye).reshape(
        depth, POOL_WIDTH, POOL_WIDTH).astype(BF16)
    w_t, w_sf, w_sb, w_hf, w_hb, lam = _ssm_weights(
        ssm_a_re.astype(F32), ssm_a_im.astype(F32), ssm_log_dt.astype(F32), ssm_b_re.astype(F32),
        ssm_b_im.astype(F32), ssm_c_re.astype(F32), ssm_c_im.astype(F32))
    bias = jnp.asarray(_attn_bias())

    x2d = x.reshape(bsz * seq, D_MODEL)
    for l in range(depth):
        proj = _inproj(x2d, pre_norm_g[l].reshape(1, D_MODEL), w_in_b[l])
        zp = _to_chunk_layout(proj, bsz, seq)
        hf, hb = _ssm_scan(zp, w_sf[l], w_sb[l], lam[l])
        yp = _ssm_out(zp, hf, hb, w_t[l], w_hf[l], w_hb[l])
        ypre = _from_chunk_layout(yp, bsz, seq)
        ypool = _pool(proj, pool_bd[l], pool_scale[l].reshape(1, POOL_WIDTH), bsz, seq)
        yattn = _attention(proj, attn_sink[l].astype(F32), bias, bsz, seq)
        x2d = _out_proj(x2d, ypre, proj, ypool, yattn, ssm_d[l].reshape(1, SSM_WIDTH), glu_w_b[l],
                        ssm_glu_b[l].reshape(1, 2 * SSM_WIDTH), w_out_b[l],
                        post_norm_g[l].reshape(1, D_MODEL))
    return x2d.reshape(bsz, seq, D_MODEL)
```

```python
import functools
import math

import numpy as np
import jax
import jax.numpy as jnp
from jax import lax
from jax.experimental import pallas as pl
from jax.experimental.pallas import tpu as pltpu

F32 = jnp.float32
BF16 = jnp.bfloat16
HIGHEST = lax.Precision.HIGHEST

D_MODEL = 1024
SSM_WIDTH = 256
POOL_WIDTH = 256
ATTN_WIDTH = 512
SSM_GROUP = 16
SSM_GROUPS = SSM_WIDTH // SSM_GROUP
SSM_PAIRS = SSM_GROUPS // 2
SSM_STATE = 64
POOL_WINDOWS = (2, 4, 8, 16)
POOL_GROUP = POOL_WIDTH // len(POOL_WINDOWS)
POOL_HALO = 8
HEAD_DIM = 64
N_HEADS = ATTN_WIDTH // HEAD_DIM
N_KV_HEADS = 2
KV_WIDTH = N_KV_HEADS * HEAD_DIM
Q_PER_KV = N_HEADS // N_KV_HEADS
WINDOW = 128
BLOCK = 128
RMS_EPS = 1e-6
NEG_BIG = -1e30

SSM_CHUNK = 16
CHUNK_WIDTH = SSM_CHUNK * SSM_GROUP
PAIR_WIDTH = 2 * CHUNK_WIDTH
SUBLANES = 8
LANES = 128
PIECE = 2 * SSM_GROUP

COL_SU, COL_SG, COL_PU, COL_PG = 0, 256, 512, 768
COL_Q, COL_AG, COL_K, COL_V = 1024, 1536, 2048, 2176
IN_WIDTH = COL_V + KV_WIDTH


def _in_col_permutation():
    idx = np.arange(IN_WIDTH)
    ref_q, ref_k, ref_v, ref_ag = 1024, 1536, 1664, 1792
    return np.concatenate([idx[:ref_k], idx[ref_ag:ref_ag + ATTN_WIDTH],
                           idx[ref_k:ref_k + KV_WIDTH], idx[ref_v:ref_v + KV_WIDTH]])


def _lane_group_masks(rows):
    lane = lax.broadcasted_iota(jnp.int32, (rows, LANES), 1)
    return [(lane >= k * PIECE) & (lane < (k + 1) * PIECE) for k in range(LANES // PIECE)]


def _to_chunk_rows(su_ref, z_ref, n_chunks):
    masks = _lane_group_masks(n_chunks)
    per_vreg = LANES // PIECE
    rows = [[su_ref[c, pl.ds(i, n_chunks, stride=SSM_CHUNK), :] for c in range(SSM_WIDTH // LANES)]
            for i in range(SSM_CHUNK)]
    for m in range(SSM_PAIRS):
        col, src_off = divmod(m * PIECE, LANES)
        for t in range(PAIR_WIDTH // LANES):
            acc = None
            for k in range(per_vreg):
                r = pltpu.roll(rows[t * per_vreg + k][col], (k * PIECE - src_off) % LANES, axis=1)
                acc = r if acc is None else jnp.where(masks[k], r, acc)
            z_ref[m, :, t * LANES:(t + 1) * LANES] = acc.astype(BF16)


def _from_chunk_rows(y_ref, out_ref, n_chunks):
    masks = _lane_group_masks(n_chunks)
    per_vreg = LANES // PIECE
    for t in range(PAIR_WIDTH // LANES):
        src = [y_ref[m, :, t * LANES:(t + 1) * LANES].astype(F32) for m in range(SSM_PAIRS)]
        for k in range(per_vreg):
            j = t * per_vreg + k
            for col in range(SSM_WIDTH // LANES):
                acc = None
                for mm in range(per_vreg):
                    r = pltpu.roll(src[col * per_vreg + mm], (mm * PIECE - k * PIECE) % LANES, axis=1)
                    acc = r if acc is None else jnp.where(masks[mm], r, acc)
                out_ref[col, pl.ds(j, n_chunks, stride=SSM_CHUNK), :] = acc


def _inproj_kernel(x_ref, g_ref, w_ref, o_ref, z_ref, h_ref, su_ref):
    x = x_ref[...]
    ms = jnp.mean(x * x, axis=-1, keepdims=True)
    h_ref[...] = (x * lax.rsqrt(ms + RMS_EPS) * g_ref[...]).astype(BF16)
    n_tile = 256
    for j in range(IN_WIDTH // n_tile):
        sl = slice(j * n_tile, (j + 1) * n_tile)
        r = jnp.dot(h_ref[...], w_ref[:, sl], preferred_element_type=F32)
        o_ref[:, sl] = r.astype(BF16)
        if j * n_tile == COL_SU:
            for c in range(SSM_WIDTH // LANES):
                su_ref[c] = r[:, c * LANES:(c + 1) * LANES]
    _to_chunk_rows(su_ref, z_ref, x_ref.shape[0] // SSM_CHUNK)


def _inproj(x2d, g, w, bsz, seq, tm=512):
    n_tok = x2d.shape[0]
    tiles_per_seq = seq // tm
    cpt = tm // SSM_CHUNK
    return pl.pallas_call(
        _inproj_kernel,
        grid=(n_tok // tm,),
        in_specs=[pl.BlockSpec((tm, D_MODEL), lambda i: (i, 0)),
                  pl.BlockSpec((1, D_MODEL), lambda i: (0, 0)),
                  pl.BlockSpec((D_MODEL, IN_WIDTH), lambda i: (0, 0))],
        out_specs=[pl.BlockSpec((tm, IN_WIDTH), lambda i: (i, 0)),
                   pl.BlockSpec((SSM_PAIRS, cpt, PAIR_WIDTH),
                                lambda i: (0, i % tiles_per_seq, i // tiles_per_seq))],
        out_shape=[jax.ShapeDtypeStruct((n_tok, IN_WIDTH), BF16),
                   jax.ShapeDtypeStruct((SSM_PAIRS, seq // SSM_CHUNK, bsz * PAIR_WIDTH), BF16)],
        scratch_shapes=[pltpu.VMEM((tm, D_MODEL), BF16),
                        pltpu.VMEM((SSM_WIDTH // LANES, tm, LANES), F32)],
        compiler_params=pltpu.CompilerParams(dimension_semantics=("arbitrary",),
                                             vmem_limit_bytes=48 * 1024 * 1024),
        name="inproj",
    )(x2d, g, w)


def _ssm_weights(a_re, a_im, log_dt, b_re, b_im, c_re, c_im):
    depth = a_re.shape[0]
    T = SSM_CHUNK
    dt = jnp.exp(log_dt)[..., None]
    ar, ai = a_re * dt, a_im * dt
    mag = jnp.exp(ar)
    lb_re, lb_im = mag * jnp.cos(ai), mag * jnp.sin(ai)
    den = a_re * a_re + a_im * a_im
    num_re = lb_re - 1.0
    coef_re = (num_re * a_re + lb_im * a_im) / den
    coef_im = (lb_im * a_re - num_re * a_im) / den
    bb_re = coef_re[..., None] * b_re - coef_im[..., None] * b_im
    bb_im = coef_re[..., None] * b_im + coef_im[..., None] * b_re
    tau = jnp.arange(T + 1, dtype=F32)[:, None, None]
    pw_mag = jnp.exp(tau * ar[:, :, None])
    pw_re = pw_mag * jnp.cos(tau * ai[:, :, None])
    pw_im = pw_mag * jnp.sin(tau * ai[:, :, None])
    m_re = c_re[:, :, None] * pw_re[..., None, :] - c_im[:, :, None] * pw_im[..., None, :]
    m_im = c_re[:, :, None] * pw_im[..., None, :] + c_im[:, :, None] * pw_re[..., None, :]
    kern = (jnp.einsum('dstgqn,dsgnp->dstgqp', m_re, bb_re, precision=HIGHEST)
            - jnp.einsum('dstgqn,dsgnp->dstgqp', m_im, bb_im, precision=HIGHEST))

    ii = jnp.arange(T)[:, None]
    jj = jnp.arange(T)[None, :]
    lag_f = jnp.clip(jj - ii, 0, T)
    lag_b = jnp.clip(ii - jj, 0, T)
    tf = kern[:, 0][:, lag_f] * (jj >= ii).astype(F32)[None, :, :, None, None, None]
    tb = kern[:, 1][:, lag_b] * (ii >= jj).astype(F32)[None, :, :, None, None, None]
    eye2 = jnp.eye(2, dtype=F32)
    toep = (tf + tb).reshape(depth, T, T, SSM_PAIRS, 2, SSM_GROUP, SSM_GROUP)
    w_t = jnp.einsum('dijmeqp,ef->dmiepjfq', toep, eye2).reshape(depth, SSM_PAIRS, PAIR_WIDTH, PAIR_WIDTH)

    def pair_cols(a):
        return a.reshape((depth, SSM_PAIRS, 2) + a.shape[2:])

    def state_in(direction, expo):
        p_re = pw_re[:, direction][:, expo]
        p_im = pw_im[:, direction][:, expo]
        br = jnp.transpose(bb_re[:, direction], (0, 1, 3, 2))
        bi = jnp.transpose(bb_im[:, direction], (0, 1, 3, 2))
        a_r = (jnp.transpose(p_re, (0, 2, 1, 3))[:, :, :, None] * br[:, :, None]
               - jnp.transpose(p_im, (0, 2, 1, 3))[:, :, :, None] * bi[:, :, None])
        a_i = (jnp.transpose(p_re, (0, 2, 1, 3))[:, :, :, None] * bi[:, :, None]
               + jnp.transpose(p_im, (0, 2, 1, 3))[:, :, :, None] * br[:, :, None])
        a = jnp.stack([pair_cols(a_r), pair_cols(a_i)], axis=5)
        w = jnp.einsum('dmeipcn,ef->dmiepcfn', a, eye2)
        return w.reshape(depth, SSM_PAIRS, PAIR_WIDTH, 4 * SSM_STATE)

    w_sf = state_in(0, jnp.arange(T - 1, -1, -1))
    w_sb = state_in(1, jnp.arange(T))

    def state_out(direction, expo):
        mr = jnp.transpose(m_re[:, direction][:, expo], (0, 2, 4, 1, 3))
        mi = jnp.transpose(m_im[:, direction][:, expo], (0, 2, 4, 1, 3))
        a = jnp.stack([pair_cols(mr), pair_cols(-mi)], axis=2)
        w = jnp.einsum('dmcenjq,ef->dmcenjfq', a, eye2)
        return w.reshape(depth, SSM_PAIRS, 4 * SSM_STATE, PAIR_WIDTH)

    w_hf = state_out(0, jnp.arange(1, T + 1))
    w_hb = state_out(1, jnp.arange(T, 0, -1))

    def lam_pairs(p):
        p = p.reshape(depth, SSM_PAIRS, 1, 2 * SSM_STATE)
        return jnp.broadcast_to(p, (depth, SSM_PAIRS, SUBLANES, 2 * SSM_STATE))

    lam = jnp.stack([lam_pairs(pw_re[:, 0, T]), lam_pairs(pw_im[:, 0, T]),
                     lam_pairs(pw_re[:, 1, T]), lam_pairs(pw_im[:, 1, T])], axis=1)
    return (w_t.astype(BF16), w_sf.astype(BF16), w_sb.astype(BF16),
            w_hf.astype(BF16), w_hb.astype(BF16), lam)


def _ssm_scan_kernel(zf_ref, zb_ref, wsf_ref, wsb_ref, lam_ref, hf_out, hb_out,
                     sfr, sfi, sbr, sbi, carry):
    rc = zf_ref.shape[1]
    half = 2 * SSM_STATE

    @pl.when(pl.program_id(0) == 0)
    def _():
        carry[...] = jnp.zeros_like(carry)

    for m in range(SSM_PAIRS):
        s = jnp.dot(zf_ref[m], wsf_ref[m], preferred_element_type=F32)
        sfr[m] = s[:, :half]
        sfi[m] = s[:, half:]
        s = jnp.dot(zb_ref[m], wsb_ref[m], preferred_element_type=F32)
        sbr[m] = s[:, :half]
        sbi[m] = s[:, half:]

    lfr, lfi, lbr, lbi = lam_ref[0], lam_ref[1], lam_ref[2], lam_ref[3]
    n_steps = rc // SUBLANES

    def body(k, c):
        hfr, hfi, hbr, hbi = c
        r = pl.multiple_of(k * SUBLANES, SUBLANES)
        hf_out[:, pl.ds(r, SUBLANES), :half] = hfr
        hf_out[:, pl.ds(r, SUBLANES), half:] = hfi
        nfr = lfr * hfr - lfi * hfi + sfr[:, pl.ds(r, SUBLANES), :]
        nfi = lfr * hfi + lfi * hfr + sfi[:, pl.ds(r, SUBLANES), :]
        rb = pl.multiple_of(rc - SUBLANES - k * SUBLANES, SUBLANES)
        hb_out[:, pl.ds(rb, SUBLANES), :half] = hbr
        hb_out[:, pl.ds(rb, SUBLANES), half:] = hbi
        nbr = lbr * hbr - lbi * hbi + sbr[:, pl.ds(rb, SUBLANES), :]
        nbi = lbr * hbi + lbi * hbr + sbi[:, pl.ds(rb, SUBLANES), :]
        return nfr, nfi, nbr, nbi

    c = lax.fori_loop(0, n_steps, body, (carry[0], carry[1], carry[2], carry[3]))
    carry[0], carry[1], carry[2], carry[3] = c


def _ssm_scan(zp, w_sf, w_sb, lam, rc=256):
    n_rows = zp.shape[1]
    n_tiles = n_rows // rc
    half = 2 * SSM_STATE
    out = jax.ShapeDtypeStruct((SSM_PAIRS, n_rows, 2 * half), F32)
    s_scratch = pltpu.VMEM((SSM_PAIRS, rc, half), F32)
    return pl.pallas_call(
        _ssm_scan_kernel,
        grid=(n_tiles,),
        in_specs=[pl.BlockSpec((SSM_PAIRS, rc, PAIR_WIDTH), lambda i: (0, i, 0)),
                  pl.BlockSpec((SSM_PAIRS, rc, PAIR_WIDTH), lambda i: (0, n_tiles - 1 - i, 0)),
                  pl.BlockSpec((SSM_PAIRS, PAIR_WIDTH, 2 * half), lambda i: (0, 0, 0)),
                  pl.BlockSpec((SSM_PAIRS, PAIR_WIDTH, 2 * half), lambda i: (0, 0, 0)),
                  pl.BlockSpec((4, SSM_PAIRS, SUBLANES, half), lambda i: (0, 0, 0, 0))],
        out_specs=[pl.BlockSpec((SSM_PAIRS, rc, 2 * half), lambda i: (0, i, 0)),
                   pl.BlockSpec((SSM_PAIRS, rc, 2 * half), lambda i: (0, n_tiles - 1 - i, 0))],
        out_shape=[out, out],
        scratch_shapes=[s_scratch, s_scratch, s_scratch, s_scratch,
                        pltpu.VMEM((4, SSM_PAIRS, SUBLANES, half), F32)],
        compiler_params=pltpu.CompilerParams(dimension_semantics=("arbitrary",),
                                             vmem_limit_bytes=48 * 1024 * 1024),
        name="ssm_scan",
    )(zp, zp, w_sf, w_sb, lam)


def _ssm_out_kernel(z_ref, hf_ref, hb_ref, wt_ref, whf_ref, whb_ref, y_ref):
    y = jnp.dot(z_ref[...], wt_ref[...], preferred_element_type=F32)
    y = y + jnp.dot(hf_ref[...].astype(BF16), whf_ref[...], preferred_element_type=F32)
    y = y + jnp.dot(hb_ref[...].astype(BF16), whb_ref[...], preferred_element_type=F32)
    y_ref[...] = y.astype(BF16)


def _ssm_out(zp, hf, hb, w_t, w_hf, w_hb, rt=512):
    n_rows = zp.shape[1]
    half2 = 4 * SSM_STATE
    return pl.pallas_call(
        _ssm_out_kernel,
        grid=(SSM_PAIRS, n_rows // rt),
        in_specs=[pl.BlockSpec((None, rt, PAIR_WIDTH), lambda m, i: (m, i, 0)),
                  pl.BlockSpec((None, rt, half2), lambda m, i: (m, i, 0)),
                  pl.BlockSpec((None, rt, half2), lambda m, i: (m, i, 0)),
                  pl.BlockSpec((None, PAIR_WIDTH, PAIR_WIDTH), lambda m, i: (m, 0, 0)),
                  pl.BlockSpec((None, half2, PAIR_WIDTH), lambda m, i: (m, 0, 0)),
                  pl.BlockSpec((None, half2, PAIR_WIDTH), lambda m, i: (m, 0, 0))],
        out_specs=pl.BlockSpec((None, rt, PAIR_WIDTH), lambda m, i: (m, i, 0)),
        out_shape=jax.ShapeDtypeStruct((SSM_PAIRS, n_rows, PAIR_WIDTH), BF16),
        compiler_params=pltpu.CompilerParams(dimension_semantics=("arbitrary", "arbitrary")),
        name="ssm_out",
    )(zp, hf, hb, w_t, w_hf, w_hb)


def _pool_kernel(pu_ref, pg_ref, w_ref, scale_ref, o_ref, pad_ref, *, seq, rows):
    halo = POOL_HALO
    zeros = jnp.zeros((halo, POOL_WIDTH), F32)
    pad_ref[0:halo, :] = zeros
    pad_ref[seq + halo:seq + 2 * halo, :] = zeros
    pad_ref[halo:seq + halo, :] = pu_ref[...].astype(F32)

    lane = lax.broadcasted_iota(jnp.int32, (rows, POOL_WIDTH), 1)
    half_win = jnp.where(lane < POOL_GROUP, 1,
                         jnp.where(lane < 2 * POOL_GROUP, 2,
                                   jnp.where(lane < 3 * POOL_GROUP, 4, 8)))
    row = lax.broadcasted_iota(jnp.int32, (rows, POOL_WIDTH), 0)

    ext = rows + 2 * halo

    def shift(a, d):
        return pltpu.roll(a, (-d) % ext, axis=0)

    def body(k, _):
        base = pl.multiple_of(k * rows, rows)
        xs = pad_ref[pl.ds(base, ext), :]
        w2 = shift(xs, -1) + xs
        w4 = shift(w2, -1) + shift(w2, 1)
        w8 = shift(w4, -2) + shift(w4, 2)
        w16 = shift(w8, -4) + shift(w8, 4)
        centre = xs[halo:halo + rows]
        total = jnp.where(half_win == 1, w2[halo:halo + rows],
                          jnp.where(half_win == 2, w4[halo:halo + rows],
                                    jnp.where(half_win == 4, w8[halo:halo + rows],
                                              w16[halo:halo + rows])))
        t = row + base
        count = jnp.minimum(t + half_win, seq) - jnp.maximum(t - half_win, 0)
        diff = total / count.astype(F32) - centre
        y = jnp.dot(diff.astype(BF16), w_ref[...], preferred_element_type=F32)
        g = pg_ref[pl.ds(base, rows), :].astype(F32)
        o_ref[pl.ds(base, rows), :] = (y * scale_ref[...] * (g * jax.nn.sigmoid(g))).astype(BF16)
        return 0

    lax.fori_loop(0, seq // rows, body, 0)


def _pool(proj, w_bd, scale, bsz, seq, rows=256):
    kern = functools.partial(_pool_kernel, seq=seq, rows=rows)
    return pl.pallas_call(
        kern,
        grid=(bsz,),
        in_specs=[pl.BlockSpec((seq, POOL_WIDTH), lambda b: (b, COL_PU // POOL_WIDTH)),
                  pl.BlockSpec((seq, POOL_WIDTH), lambda b: (b, COL_PG // POOL_WIDTH)),
                  pl.BlockSpec((POOL_WIDTH, POOL_WIDTH), lambda b: (0, 0)),
                  pl.BlockSpec((1, POOL_WIDTH), lambda b: (0, 0))],
        out_specs=pl.BlockSpec((seq, POOL_WIDTH), lambda b: (b, 0)),
        out_shape=jax.ShapeDtypeStruct((bsz * seq, POOL_WIDTH), BF16),
        scratch_shapes=[pltpu.VMEM((seq + 2 * POOL_HALO, POOL_WIDTH), F32)],
        compiler_params=pltpu.CompilerParams(dimension_semantics=("arbitrary",)),
        name="pool",
    )(proj, proj, w_bd, scale)


def _attn_bias():
    slopes = np.exp2(-8.0 * np.arange(1, N_HEADS + 1, dtype=np.float32) / N_HEADS).astype(np.float32)
    qpos = np.arange(BLOCK)[:, None]
    kpos = np.arange(3 * BLOCK)[None, :] - BLOCK
    dist = np.abs(qpos - kpos)
    bias = -slopes[:, None, None] * dist.astype(np.float32)[None]
    return np.where(dist[None] <= WINDOW, bias, np.float32(NEG_BIG)).astype(np.float32)


def _attn_kernel(sink_ref, q_ref, kp_ref, kc_ref, kn_ref, vp_ref, vc_ref, vn_ref, ag_ref, bias_ref,
                 o_ref, *, n_blocks, seq):
    n = pl.program_id(1)
    kk = jnp.concatenate([kp_ref[...], kc_ref[...], kn_ref[...]], axis=0)
    vv = jnp.concatenate([vp_ref[...], vc_ref[...], vn_ref[...]], axis=0)
    kpos = lax.broadcasted_iota(jnp.int32, (1, 3 * BLOCK), 1) + (n - 1) * BLOCK
    edge = jnp.where((kpos >= 0) & (kpos < seq), 0.0, NEG_BIG).astype(F32)
    scale = HEAD_DIM ** -0.5
    for h in range(N_HEADS):
        kv = h // Q_PER_KV
        qh = q_ref[:, h * HEAD_DIM:(h + 1) * HEAD_DIM]
        kh = kk[:, kv * HEAD_DIM:(kv + 1) * HEAD_DIM]
        vh = vv[:, kv * HEAD_DIM:(kv + 1) * HEAD_DIM]
        s = lax.dot_general(qh, kh, (((1,), (1,)), ((), ())), preferred_element_type=F32)
        s = s * scale + bias_ref[h] + edge
        sink = sink_ref[h]
        m = jnp.maximum(jnp.max(s, axis=-1, keepdims=True), sink)
        p = jnp.exp(s - m)
        denom = jnp.sum(p, axis=-1, keepdims=True) + jnp.exp(sink - m)
        o = jnp.dot(p.astype(BF16), vh, preferred_element_type=F32) / denom
        g = ag_ref[:, h * HEAD_DIM:(h + 1) * HEAD_DIM].astype(F32)
        o_ref[:, h * HEAD_DIM:(h + 1) * HEAD_DIM] = (o * (g * jax.nn.sigmoid(g))).astype(BF16)


def _attention(proj, sink, bias, bsz, seq):
    nb = seq // BLOCK
    kern = functools.partial(_attn_kernel, n_blocks=nb, seq=seq)
    kcol, vcol = COL_K // KV_WIDTH, COL_V // KV_WIDTH

    def kv_spec(col, off):
        return pl.BlockSpec((BLOCK, KV_WIDTH),
                            lambda b, n: (b * nb + jnp.clip(n + off, 0, nb - 1), col))

    return pl.pallas_call(
        kern,
        grid=(bsz, nb),
        in_specs=[pl.BlockSpec(memory_space=pltpu.SMEM),
                  pl.BlockSpec((BLOCK, ATTN_WIDTH), lambda b, n: (b * nb + n, COL_Q // ATTN_WIDTH)),
                  kv_spec(kcol, -1), kv_spec(kcol, 0), kv_spec(kcol, 1),
                  kv_spec(vcol, -1), kv_spec(vcol, 0), kv_spec(vcol, 1),
                  pl.BlockSpec((BLOCK, ATTN_WIDTH), lambda b, n: (b * nb + n, COL_AG // ATTN_WIDTH)),
                  pl.BlockSpec((N_HEADS, BLOCK, 3 * BLOCK), lambda b, n: (0, 0, 0))],
        out_specs=pl.BlockSpec((BLOCK, ATTN_WIDTH), lambda b, n: (b * nb + n, 0)),
        out_shape=jax.ShapeDtypeStruct((bsz * seq, ATTN_WIDTH), BF16),
        compiler_params=pltpu.CompilerParams(dimension_semantics=("arbitrary", "arbitrary")),
        name="attention",
    )(sink, proj, proj, proj, proj, proj, proj, proj, proj, bias)


def _out_kernel(x_ref, yp_ref, su_ref, sg_ref, ypool_ref, yattn_ref, d_ref, gw_ref, gb_ref,
                wo_ref, pg_ref, o_ref, ypre_ref):
    _from_chunk_rows(yp_ref, ypre_ref, x_ref.shape[0] // SSM_CHUNK)
    u = su_ref[...].astype(F32)
    ypre = jnp.concatenate([ypre_ref[c] for c in range(SSM_WIDTH // LANES)], axis=1)
    y = jax.nn.gelu(ypre + d_ref[...] * u)
    z = jnp.dot(y.astype(BF16), gw_ref[...], preferred_element_type=F32) + gb_ref[...]
    sg = sg_ref[...].astype(F32)
    y_ssm = z[:, :SSM_WIDTH] * jax.nn.sigmoid(z[:, SSM_WIDTH:]) * (sg * jax.nn.sigmoid(sg))
    acc = jnp.dot(y_ssm.astype(BF16), wo_ref[0:SSM_WIDTH, :], preferred_element_type=F32)
    acc = acc + jnp.dot(ypool_ref[...], wo_ref[SSM_WIDTH:SSM_WIDTH + POOL_WIDTH, :],
                        preferred_element_type=F32)
    acc = acc + jnp.dot(yattn_ref[...], wo_ref[SSM_WIDTH + POOL_WIDTH:, :], preferred_element_type=F32)
    ms = jnp.mean(acc * acc, axis=-1, keepdims=True)
    o_ref[...] = x_ref[...] + acc * lax.rsqrt(ms + RMS_EPS) * pg_ref[...]


def _out_proj(x2d, yp, proj, ypool, yattn, d, glu_w, glu_b, w_out, post_g, seq, tm=512):
    n_tok = x2d.shape[0]
    tiles_per_seq = seq // tm
    row = lambda c: (lambda i: (i, c))
    const = lambda i: (0, 0)
    return pl.pallas_call(
        _out_kernel,
        grid=(n_tok // tm,),
        in_specs=[pl.BlockSpec((tm, D_MODEL), row(0)),
                  pl.BlockSpec((SSM_PAIRS, tm // SSM_CHUNK, PAIR_WIDTH),
                               lambda i: (0, i % tiles_per_seq, i // tiles_per_seq)),
                  pl.BlockSpec((tm, SSM_WIDTH), row(COL_SU // SSM_WIDTH)),
                  pl.BlockSpec((tm, SSM_WIDTH), row(COL_SG // SSM_WIDTH)),
                  pl.BlockSpec((tm, POOL_WIDTH), row(0)),
                  pl.BlockSpec((tm, ATTN_WIDTH), row(0)),
                  pl.BlockSpec((1, SSM_WIDTH), const),
                  pl.BlockSpec((SSM_WIDTH, 2 * SSM_WIDTH), const),
                  pl.BlockSpec((1, 2 * SSM_WIDTH), const),
                  pl.BlockSpec((D_MODEL, D_MODEL), const),
                  pl.BlockSpec((1, D_MODEL), const)],
        out_specs=pl.BlockSpec((tm, D_MODEL), row(0)),
        out_shape=jax.ShapeDtypeStruct((n_tok, D_MODEL), F32),
        scratch_shapes=[pltpu.VMEM((SSM_WIDTH // LANES, tm, LANES), F32)],
        compiler_params=pltpu.CompilerParams(dimension_semantics=("arbitrary",),
                                             vmem_limit_bytes=48 * 1024 * 1024),
        name="out_proj",
    )(x2d, yp, proj, proj, ypool, yattn, d, glu_w, glu_b, w_out, post_g)


def kernel(x, pre_norm_g, w_in, ssm_a_re, ssm_a_im, ssm_log_dt, ssm_b_re, ssm_b_im, ssm_c_re, ssm_c_im,
           ssm_d, ssm_glu_w, ssm_glu_b, pool_w, pool_scale, attn_sink, w_out, post_norm_g):
    bsz, seq, _ = x.shape
    depth = w_in.shape[0]
    assert seq % BLOCK == 0 and seq % SSM_CHUNK == 0 and bsz == SUBLANES

    w_in_b = w_in[:, :, _in_col_permutation()].astype(BF16)
    w_out_b = w_out.astype(BF16)
    glu_w_b = ssm_glu_w.astype(BF16)
    eye = jnp.eye(len(POOL_WINDOWS), dtype=F32)
    pool_bd = jnp.einsum('lgcd,gh->lgchd', pool_w.astype(F32), eye).reshape(
        depth, POOL_WIDTH, POOL_WIDTH).astype(BF16)
    w_t, w_sf, w_sb, w_hf, w_hb, lam = _ssm_weights(
        ssm_a_re.astype(F32), ssm_a_im.astype(F32), ssm_log_dt.astype(F32), ssm_b_re.astype(F32),
        ssm_b_im.astype(F32), ssm_c_re.astype(F32), ssm_c_im.astype(F32))
    bias = jnp.asarray(_attn_bias())

    x2d = x.reshape(bsz * seq, D_MODEL)
    n_chunks = seq // SSM_CHUNK
    for l in range(depth):
        proj, z = _inproj(x2d, pre_norm_g[l].reshape(1, D_MODEL), w_in_b[l], bsz, seq)
        zp = z.reshape(SSM_PAIRS, n_chunks * bsz, PAIR_WIDTH)
        hf, hb = _ssm_scan(zp, w_sf[l], w_sb[l], lam[l])
        yp = _ssm_out(zp, hf, hb, w_t[l], w_hf[l], w_hb[l])
        yp = yp.reshape(SSM_PAIRS, n_chunks, bsz * PAIR_WIDTH)
        ypool = _pool(proj, pool_bd[l], pool_scale[l].reshape(1, POOL_WIDTH), bsz, seq)
        yattn = _attention(proj, attn_sink[l].astype(F32), bias, bsz, seq)
        x2d = _out_proj(x2d, yp, proj, ypool, yattn, ssm_d[l].reshape(1, SSM_WIDTH), glu_w_b[l],
                        ssm_glu_b[l].reshape(1, 2 * SSM_WIDTH), w_out_b[l],
                        post_norm_g[l].reshape(1, D_MODEL), seq)
    return x2d.reshape(bsz, seq, D_MODEL)
```

```python
import functools
import math

import numpy as np
import jax
import jax.numpy as jnp
from jax import lax
from jax.experimental import pallas as pl
from jax.experimental.pallas import tpu as pltpu

F32 = jnp.float32
BF16 = jnp.bfloat16
HIGHEST = lax.Precision.HIGHEST

D_MODEL = 1024
SSM_WIDTH = 256
POOL_WIDTH = 256
ATTN_WIDTH = 512
SSM_GROUP = 16
SSM_GROUPS = SSM_WIDTH // SSM_GROUP
SSM_PAIRS = SSM_GROUPS // 2
SSM_STATE = 64
POOL_WINDOWS = (2, 4, 8, 16)
POOL_GROUP = POOL_WIDTH // len(POOL_WINDOWS)
POOL_HALO = 8
HEAD_DIM = 64
N_HEADS = ATTN_WIDTH // HEAD_DIM
N_KV_HEADS = 2
KV_WIDTH = N_KV_HEADS * HEAD_DIM
Q_PER_KV = N_HEADS // N_KV_HEADS
WINDOW = 128
BLOCK = 128
RMS_EPS = 1e-6
NEG_BIG = -1e30
LOG2_E = math.log2(math.e)

SSM_CHUNK = 16
CHUNK_WIDTH = SSM_CHUNK * SSM_GROUP
PAIR_WIDTH = 2 * CHUNK_WIDTH
SUBLANES = 8
TOKENS_PER_TILE = 64
LANES = 128
PIECE = 2 * SSM_GROUP

COL_SU, COL_SG, COL_PU, COL_PG = 0, 256, 512, 768
COL_Q, COL_AG, COL_K, COL_V = 1024, 1536, 2048, 2176
IN_WIDTH = COL_V + KV_WIDTH


def _in_col_permutation():
    idx = np.arange(IN_WIDTH)
    ref_q, ref_k, ref_v, ref_ag = 1024, 1536, 1664, 1792
    return np.concatenate([idx[:ref_k], idx[ref_ag:ref_ag + ATTN_WIDTH],
                           idx[ref_k:ref_k + KV_WIDTH], idx[ref_v:ref_v + KV_WIDTH]])


def _lane_group_masks(rows):
    lane = lax.broadcasted_iota(jnp.int32, (rows, LANES), 1)
    return [(lane >= k * PIECE) & (lane < (k + 1) * PIECE) for k in range(LANES // PIECE)]


def _to_chunk_rows(su_ref, z_ref, bsz, tb):
    cpt = tb // SSM_CHUNK
    masks = _lane_group_masks(cpt * bsz)
    per_vreg = LANES // PIECE
    rows = [[jnp.concatenate([su_ref[col, pl.ds(c * SSM_CHUNK + i, bsz, stride=tb), :] for c in range(cpt)],
                             axis=0) for col in range(SSM_WIDTH // LANES)]
            for i in range(SSM_CHUNK)]
    for m in range(SSM_PAIRS):
        col, src_off = divmod(m * PIECE, LANES)
        for t in range(PAIR_WIDTH // LANES):
            acc = None
            for k in range(per_vreg):
                r = pltpu.roll(rows[t * per_vreg + k][col], (k * PIECE - src_off) % LANES, axis=1)
                acc = r if acc is None else jnp.where(masks[k], r, acc)
            z_ref[m, :, t * LANES:(t + 1) * LANES] = acc.astype(BF16)


def _from_chunk_rows(y_ref, out_ref, bsz, tb):
    cpt = tb // SSM_CHUNK
    masks = _lane_group_masks(cpt * bsz)
    per_vreg = LANES // PIECE
    for t in range(PAIR_WIDTH // LANES):
        src = [y_ref[m, :, t * LANES:(t + 1) * LANES].astype(F32) for m in range(SSM_PAIRS)]
        for k in range(per_vreg):
            j = t * per_vreg + k
            for col in range(SSM_WIDTH // LANES):
                acc = None
                for mm in range(per_vreg):
                    r = pltpu.roll(src[col * per_vreg + mm], (mm * PIECE - k * PIECE) % LANES, axis=1)
                    acc = r if acc is None else jnp.where(masks[mm], r, acc)
                for c in range(cpt):
                    out_ref[col, pl.ds(c * SSM_CHUNK + j, bsz, stride=tb), :] = acc[c * bsz:(c + 1) * bsz]


def _inproj_kernel(x_ref, g_ref, w_ref, o_ref, z_ref, h_ref, su_ref):
    bsz, tb, _ = x_ref.shape
    x = x_ref[...].reshape(bsz * tb, D_MODEL)
    ms = jnp.mean(x * x, axis=-1, keepdims=True)
    h_ref[...] = (x * lax.rsqrt(ms + RMS_EPS) * g_ref[...]).astype(BF16)
    n_tile = 256
    for j in range(IN_WIDTH // n_tile):
        sl = slice(j * n_tile, (j + 1) * n_tile)
        r = jnp.dot(h_ref[...], w_ref[:, sl], preferred_element_type=F32)
        o_ref[:, :, sl] = r.astype(BF16).reshape(bsz, tb, n_tile)
        if j * n_tile == COL_SU:
            for c in range(SSM_WIDTH // LANES):
                su_ref[c] = r[:, c * LANES:(c + 1) * LANES]
    _to_chunk_rows(su_ref, z_ref, bsz, tb)


def _inproj(x, g, w, tb=TOKENS_PER_TILE):
    bsz, seq, _ = x.shape
    tm = bsz * tb
    rows_per_tile = tm // SSM_CHUNK
    return pl.pallas_call(
        _inproj_kernel,
        grid=(seq // tb,),
        in_specs=[pl.BlockSpec((bsz, tb, D_MODEL), lambda i: (0, i, 0)),
                  pl.BlockSpec((1, D_MODEL), lambda i: (0, 0)),
                  pl.BlockSpec((D_MODEL, IN_WIDTH), lambda i: (0, 0))],
        out_specs=[pl.BlockSpec((bsz, tb, IN_WIDTH), lambda i: (0, i, 0)),
                   pl.BlockSpec((SSM_PAIRS, rows_per_tile, PAIR_WIDTH), lambda i: (0, i, 0))],
        out_shape=[jax.ShapeDtypeStruct((bsz, seq, IN_WIDTH), BF16),
                   jax.ShapeDtypeStruct((SSM_PAIRS, bsz * seq // SSM_CHUNK, PAIR_WIDTH), BF16)],
        scratch_shapes=[pltpu.VMEM((tm, D_MODEL), BF16),
                        pltpu.VMEM((SSM_WIDTH // LANES, tm, LANES), F32)],
        compiler_params=pltpu.CompilerParams(dimension_semantics=("arbitrary",),
                                             vmem_limit_bytes=48 * 1024 * 1024),
        name="inproj",
    )(x, g, w)


def _ssm_weights(a_re, a_im, log_dt, b_re, b_im, c_re, c_im):
    depth = a_re.shape[0]
    T = SSM_CHUNK
    dt = jnp.exp(log_dt)[..., None]
    ar, ai = a_re * dt, a_im * dt
    mag = jnp.exp(ar)
    lb_re, lb_im = mag * jnp.cos(ai), mag * jnp.sin(ai)
    den = a_re * a_re + a_im * a_im
    num_re = lb_re - 1.0
    coef_re = (num_re * a_re + lb_im * a_im) / den
    coef_im = (lb_im * a_re - num_re * a_im) / den
    bb_re = coef_re[..., None] * b_re - coef_im[..., None] * b_im
    bb_im = coef_re[..., None] * b_im + coef_im[..., None] * b_re
    tau = jnp.arange(T + 1, dtype=F32)[:, None, None]
    pw_mag = jnp.exp(tau * ar[:, :, None])
    pw_re = pw_mag * jnp.cos(tau * ai[:, :, None])
    pw_im = pw_mag * jnp.sin(tau * ai[:, :, None])
    half = 2 * SSM_STATE
    pwp_re = pw_re.reshape(depth, 2, T + 1, SSM_PAIRS, half)
    pwp_im = pw_im.reshape(depth, 2, T + 1, SSM_PAIRS, half)
    eye2 = jnp.eye(2, dtype=F32)

    def pair_blocks(a):
        r = a.shape[3]
        a = a.reshape(depth, 2, SSM_PAIRS, 2, r, 1, SSM_STATE)
        return (a * eye2[:, None, :, None]).reshape(depth, 2, SSM_PAIRS, 2 * r, half)

    bp_re = pair_blocks(jnp.transpose(bb_re, (0, 1, 2, 4, 3)))
    bp_im = pair_blocks(jnp.transpose(bb_im, (0, 1, 2, 4, 3)))
    cp_re = pair_blocks(c_re)
    cp_im = pair_blocks(c_im)

    def scaled(direction, expo, x_re, x_im):
        pr = jnp.transpose(pwp_re[:, direction][:, expo], (0, 2, 1, 3))[:, :, :, None]
        pi = jnp.transpose(pwp_im[:, direction][:, expo], (0, 2, 1, 3))[:, :, :, None]
        xr, xi = x_re[:, direction][:, :, None], x_im[:, direction][:, :, None]
        return pr * xr - pi * xi, pr * xi + pi * xr

    def rows_by_step(re, im):
        return jnp.concatenate([re, im], axis=-1).reshape(depth, SSM_PAIRS, PAIR_WIDTH, 2 * half)

    w_sf = rows_by_step(*scaled(0, jnp.arange(T - 1, -1, -1), bp_re, bp_im))
    w_sb = rows_by_step(*scaled(1, jnp.arange(T), bp_re, bp_im))

    def state_out(direction, expo):
        m_re, m_im = scaled(direction, expo, cp_re, cp_im)
        return rows_by_step(m_re, -m_im)

    w_hf = state_out(0, jnp.arange(1, T + 1))
    w_hb = state_out(1, jnp.arange(T, 0, -1))

    def lag_kernels(direction):
        s_re, s_im = scaled(direction, jnp.arange(T), bp_re, bp_im)
        return (jnp.einsum('dmtxk,dmyk->dmxty', s_re, cp_re[:, direction], precision=HIGHEST)
                - jnp.einsum('dmtxk,dmyk->dmxty', s_im, cp_im[:, direction], precision=HIGHEST))

    kf, kb = lag_kernels(0), lag_kernels(1)
    by_lag = jnp.concatenate([jnp.flip(kb[:, :, :, 1:], axis=3), kf[:, :, :, :1] + kb[:, :, :, :1],
                              kf[:, :, :, 1:]], axis=3).reshape(depth, SSM_PAIRS, PIECE, (2 * T - 1) * PIECE)
    w_t = jnp.stack([by_lag[..., (T - 1 - i) * PIECE:(T - 1 - i) * PIECE + PAIR_WIDTH] for i in range(T)],
                    axis=2).reshape(depth, SSM_PAIRS, PAIR_WIDTH, PAIR_WIDTH)

    def lam_rows(p):
        return jnp.broadcast_to(p[:, :, None], (depth, SSM_PAIRS, SUBLANES, half))

    lam = jnp.stack([lam_rows(pwp_re[:, 0, T]), lam_rows(pwp_im[:, 0, T]),
                     lam_rows(pwp_re[:, 1, T]), lam_rows(pwp_im[:, 1, T])], axis=1)
    return (w_t.astype(BF16), w_sf.astype(BF16), w_sb.astype(BF16),
            w_hf.astype(BF16), w_hb.astype(BF16), lam)


def _ssm_scan_kernel(zf_ref, zb_ref, wsf_ref, wsb_ref, lam_ref, hf_out, hb_out,
                     sfr, sfi, sbr, sbi, carry):
    rc = zf_ref.shape[1]
    half = 2 * SSM_STATE

    @pl.when(pl.program_id(0) == 0)
    def _():
        carry[...] = jnp.zeros_like(carry)

    for m in range(SSM_PAIRS):
        s = jnp.dot(zf_ref[m], wsf_ref[m], preferred_element_type=F32)
        sfr[m] = s[:, :half]
        sfi[m] = s[:, half:]
        s = jnp.dot(zb_ref[m], wsb_ref[m], preferred_element_type=F32)
        sbr[m] = s[:, :half]
        sbi[m] = s[:, half:]

    lfr, lfi, lbr, lbi = lam_ref[0], lam_ref[1], lam_ref[2], lam_ref[3]
    n_steps = rc // SUBLANES

    def body(k, c):
        hfr, hfi, hbr, hbi = c
        r = pl.multiple_of(k * SUBLANES, SUBLANES)
        hf_out[:, pl.ds(r, SUBLANES), :half] = hfr
        hf_out[:, pl.ds(r, SUBLANES), half:] = hfi
        nfr = lfr * hfr - lfi * hfi + sfr[:, pl.ds(r, SUBLANES), :]
        nfi = lfr * hfi + lfi * hfr + sfi[:, pl.ds(r, SUBLANES), :]
        rb = pl.multiple_of(rc - SUBLANES - k * SUBLANES, SUBLANES)
        hb_out[:, pl.ds(rb, SUBLANES), :half] = hbr
        hb_out[:, pl.ds(rb, SUBLANES), half:] = hbi
        nbr = lbr * hbr - lbi * hbi + sbr[:, pl.ds(rb, SUBLANES), :]
        nbi = lbr * hbi + lbi * hbr + sbi[:, pl.ds(rb, SUBLANES), :]
        return nfr, nfi, nbr, nbi

    c = lax.fori_loop(0, n_steps, body, (carry[0], carry[1], carry[2], carry[3]))
    carry[0], carry[1], carry[2], carry[3] = c


def _ssm_scan(zp, w_sf, w_sb, lam, rc=256):
    n_rows = zp.shape[1]
    n_tiles = n_rows // rc
    half = 2 * SSM_STATE
    out = jax.ShapeDtypeStruct((SSM_PAIRS, n_rows, 2 * half), F32)
    s_scratch = pltpu.VMEM((SSM_PAIRS, rc, half), F32)
    return pl.pallas_call(
        _ssm_scan_kernel,
        grid=(n_tiles,),
        in_specs=[pl.BlockSpec((SSM_PAIRS, rc, PAIR_WIDTH), lambda i: (0, i, 0)),
                  pl.BlockSpec((SSM_PAIRS, rc, PAIR_WIDTH), lambda i: (0, n_tiles - 1 - i, 0)),
                  pl.BlockSpec((SSM_PAIRS, PAIR_WIDTH, 2 * half), lambda i: (0, 0, 0)),
                  pl.BlockSpec((SSM_PAIRS, PAIR_WIDTH, 2 * half), lambda i: (0, 0, 0)),
                  pl.BlockSpec((4, SSM_PAIRS, SUBLANES, half), lambda i: (0, 0, 0, 0))],
        out_specs=[pl.BlockSpec((SSM_PAIRS, rc, 2 * half), lambda i: (0, i, 0)),
                   pl.BlockSpec((SSM_PAIRS, rc, 2 * half), lambda i: (0, n_tiles - 1 - i, 0))],
        out_shape=[out, out],
        scratch_shapes=[s_scratch, s_scratch, s_scratch, s_scratch,
                        pltpu.VMEM((4, SSM_PAIRS, SUBLANES, half), F32)],
        compiler_params=pltpu.CompilerParams(dimension_semantics=("arbitrary",),
                                             vmem_limit_bytes=48 * 1024 * 1024),
        name="ssm_scan",
    )(zp, zp, w_sf, w_sb, lam)


def _ssm_out_kernel(z_ref, hf_ref, hb_ref, wt_ref, whf_ref, whb_ref, y_ref):
    nt = (((1,), (1,)), ((), ()))
    y = jnp.dot(z_ref[...], wt_ref[...], preferred_element_type=F32)
    y = y + lax.dot_general(hf_ref[...].astype(BF16), whf_ref[...], nt, preferred_element_type=F32)
    y = y + lax.dot_general(hb_ref[...].astype(BF16), whb_ref[...], nt, preferred_element_type=F32)
    y_ref[...] = y.astype(BF16)


def _ssm_out(zp, hf, hb, w_t, w_hf, w_hb, rt=512):
    n_rows = zp.shape[1]
    rt = min(rt, n_rows)
    half2 = 4 * SSM_STATE
    return pl.pallas_call(
        _ssm_out_kernel,
        grid=(SSM_PAIRS, n_rows // rt),
        in_specs=[pl.BlockSpec((None, rt, PAIR_WIDTH), lambda m, i: (m, i, 0)),
                  pl.BlockSpec((None, rt, half2), lambda m, i: (m, i, 0)),
                  pl.BlockSpec((None, rt, half2), lambda m, i: (m, i, 0)),
                  pl.BlockSpec((None, PAIR_WIDTH, PAIR_WIDTH), lambda m, i: (m, 0, 0)),
                  pl.BlockSpec((None, PAIR_WIDTH, half2), lambda m, i: (m, 0, 0)),
                  pl.BlockSpec((None, PAIR_WIDTH, half2), lambda m, i: (m, 0, 0))],
        out_specs=pl.BlockSpec((None, rt, PAIR_WIDTH), lambda m, i: (m, i, 0)),
        out_shape=jax.ShapeDtypeStruct((SSM_PAIRS, n_rows, PAIR_WIDTH), BF16),
        compiler_params=pltpu.CompilerParams(dimension_semantics=("arbitrary", "arbitrary")),
        name="ssm_out",
    )(zp, hf, hb, w_t, w_hf, w_hb)


def _pool_kernel(pu_ref, pg_ref, w_ref, scale_ref, o_ref, pad_ref, *, seq, rows):
    halo = POOL_HALO
    zeros = jnp.zeros((halo, POOL_WIDTH), F32)
    pad_ref[0:halo, :] = zeros
    pad_ref[seq + halo:seq + 2 * halo, :] = zeros
    pad_ref[halo:seq + halo, :] = pu_ref[...].astype(F32)

    lane = lax.broadcasted_iota(jnp.int32, (rows, POOL_WIDTH), 1)
    half_win = jnp.where(lane < POOL_GROUP, 1,
                         jnp.where(lane < 2 * POOL_GROUP, 2,
                                   jnp.where(lane < 3 * POOL_GROUP, 4, 8)))
    row = lax.broadcasted_iota(jnp.int32, (rows, POOL_WIDTH), 0)

    ext = rows + 2 * halo

    def shift(a, d):
        return pltpu.roll(a, (-d) % ext, axis=0)

    def body(k, _):
        base = pl.multiple_of(k * rows, rows)
        xs = pad_ref[pl.ds(base, ext), :]
        w2 = shift(xs, -1) + xs
        w4 = shift(w2, -1) + shift(w2, 1)
        w8 = shift(w4, -2) + shift(w4, 2)
        w16 = shift(w8, -4) + shift(w8, 4)
        centre = xs[halo:halo + rows]
        total = jnp.where(half_win == 1, w2[halo:halo + rows],
                          jnp.where(half_win == 2, w4[halo:halo + rows],
                                    jnp.where(half_win == 4, w8[halo:halo + rows],
                                              w16[halo:halo + rows])))
        t = row + base
        count = jnp.minimum(t + half_win, seq) - jnp.maximum(t - half_win, 0)
        diff = total / count.astype(F32) - centre
        y = jnp.dot(diff.astype(BF16), w_ref[...], preferred_element_type=F32)
        g = pg_ref[pl.ds(base, rows), :].astype(F32)
        o_ref[pl.ds(base, rows), :] = (y * scale_ref[...] * (g * jax.nn.sigmoid(g))).astype(BF16)
        return 0

    lax.fori_loop(0, seq // rows, body, 0)


def _pool(proj, w_bd, scale, bsz, seq, rows=256):
    kern = functools.partial(_pool_kernel, seq=seq, rows=rows)
    return pl.pallas_call(
        kern,
        grid=(bsz,),
        in_specs=[pl.BlockSpec((seq, POOL_WIDTH), lambda b: (b, COL_PU // POOL_WIDTH)),
                  pl.BlockSpec((seq, POOL_WIDTH), lambda b: (b, COL_PG // POOL_WIDTH)),
                  pl.BlockSpec((POOL_WIDTH, POOL_WIDTH), lambda b: (0, 0)),
                  pl.BlockSpec((1, POOL_WIDTH), lambda b: (0, 0))],
        out_specs=pl.BlockSpec((seq, POOL_WIDTH), lambda b: (b, 0)),
        out_shape=jax.ShapeDtypeStruct((bsz * seq, POOL_WIDTH), BF16),
        scratch_shapes=[pltpu.VMEM((seq + 2 * POOL_HALO, POOL_WIDTH), F32)],
        compiler_params=pltpu.CompilerParams(dimension_semantics=("arbitrary",)),
        name="pool",
    )(proj, proj, w_bd, scale)


def _attn_bias():
    slopes = np.exp2(-8.0 * np.arange(1, N_HEADS + 1, dtype=np.float32) / N_HEADS).astype(np.float32)
    qpos = np.arange(BLOCK)[:, None]
    kpos = np.arange(3 * BLOCK)[None, :] - BLOCK
    dist = np.abs(qpos - kpos)
    bias = -slopes[:, None, None] * dist.astype(np.float32)[None]
    bias = np.where(dist[None] <= WINDOW, bias * np.float32(LOG2_E), np.float32(NEG_BIG)).astype(np.float32)
    first, last = bias.copy(), bias.copy()
    first[:, :, :BLOCK] = NEG_BIG
    last[:, :, 2 * BLOCK:] = NEG_BIG
    return np.stack([first, bias, last])


def _attn_kernel(sink_ref, q_ref, kp_ref, kc_ref, kn_ref, vp_ref, vc_ref, vn_ref, ag_ref, bias_ref,
                 o_ref, s_ref, p_ref, *, n_blocks):
    n = pl.program_id(1)
    variant = jnp.where(n == 0, 0, jnp.where(n == n_blocks - 1, 2, 1))
    nt = (((1,), (1,)), ((), ()))
    lo = lax.broadcasted_iota(jnp.int32, (3 * BLOCK, KV_WIDTH), 1) < HEAD_DIM

    def per_head_halves(a_ref3):
        a = jnp.concatenate([r[...] for r in a_ref3], axis=0).astype(F32)
        swapped = pltpu.roll(a, HEAD_DIM, axis=1)
        return [[jnp.where(lo, a, 0.0).astype(BF16), jnp.where(lo, 0.0, swapped).astype(BF16)],
                [jnp.where(lo, swapped, 0.0).astype(BF16), jnp.where(lo, 0.0, a).astype(BF16)]]

    k_sel = per_head_halves((kp_ref, kc_ref, kn_ref))
    v_sel = per_head_halves((vp_ref, vc_ref, vn_ref))

    for h in range(N_HEADS):
        pair, odd = divmod(h, 2)
        qp = q_ref[:, pair * 2 * HEAD_DIM:(pair + 1) * 2 * HEAD_DIM]
        s_ref[h] = (lax.dot_general(qp, k_sel[h // Q_PER_KV][odd], nt, preferred_element_type=F32)
                    + bias_ref[variant, h])

    inv_l = []
    for h in range(N_HEADS):
        s = s_ref[h]
        sink = sink_ref[h]
        m = jnp.maximum(jnp.max(s, axis=-1, keepdims=True), sink)
        p = jnp.exp2(s - m)
        inv_l.append(1.0 / (jnp.sum(p, axis=-1, keepdims=True) + jnp.exp2(sink - m)))
        p_ref[h] = p.astype(BF16)

    lo_q = lax.broadcasted_iota(jnp.int32, (BLOCK, 2 * HEAD_DIM), 1) < HEAD_DIM
    for pair in range(N_HEADS // 2):
        kv = (2 * pair) // Q_PER_KV
        o = (jnp.dot(p_ref[2 * pair], v_sel[kv][0], preferred_element_type=F32)
             + jnp.dot(p_ref[2 * pair + 1], v_sel[kv][1], preferred_element_type=F32))
        o = o * jnp.where(lo_q, inv_l[2 * pair], inv_l[2 * pair + 1])
        sl = slice(pair * 2 * HEAD_DIM, (pair + 1) * 2 * HEAD_DIM)
        g = ag_ref[:, sl].astype(F32)
        o_ref[:, sl] = (o * (g * jax.nn.sigmoid(g))).astype(BF16)


def _attention(proj, sink, bias, bsz, seq):
    nb = seq // BLOCK
    assert nb >= 2
    kern = functools.partial(_attn_kernel, n_blocks=nb)
    kcol, vcol = COL_K // KV_WIDTH, COL_V // KV_WIDTH

    def kv_spec(col, off):
        return pl.BlockSpec((BLOCK, KV_WIDTH),
                            lambda b, n: (b * nb + jnp.clip(n + off, 0, nb - 1), col))

    return pl.pallas_call(
        kern,
        grid=(bsz, nb),
        in_specs=[pl.BlockSpec(memory_space=pltpu.SMEM),
                  pl.BlockSpec((BLOCK, ATTN_WIDTH), lambda b, n: (b * nb + n, COL_Q // ATTN_WIDTH)),
                  kv_spec(kcol, -1), kv_spec(kcol, 0), kv_spec(kcol, 1),
                  kv_spec(vcol, -1), kv_spec(vcol, 0), kv_spec(vcol, 1),
                  pl.BlockSpec((BLOCK, ATTN_WIDTH), lambda b, n: (b * nb + n, COL_AG // ATTN_WIDTH)),
                  pl.BlockSpec((3, N_HEADS, BLOCK, 3 * BLOCK), lambda b, n: (0, 0, 0, 0))],
        out_specs=pl.BlockSpec((BLOCK, ATTN_WIDTH), lambda b, n: (b * nb + n, 0)),
        out_shape=jax.ShapeDtypeStruct((bsz * seq, ATTN_WIDTH), BF16),
        scratch_shapes=[pltpu.VMEM((N_HEADS, BLOCK, 3 * BLOCK), F32),
                        pltpu.VMEM((N_HEADS, BLOCK, 3 * BLOCK), BF16)],
        compiler_params=pltpu.CompilerParams(dimension_semantics=("arbitrary", "arbitrary")),
        name="attention",
    )(sink, proj, proj, proj, proj, proj, proj, proj, proj, bias)


def _out_kernel(x_ref, yp_ref, su_ref, sg_ref, ypool_ref, yattn_ref, d_ref, gw_ref, gb_ref,
                wo_ref, pg_ref, o_ref, ypre_ref):
    bsz, tb, _ = x_ref.shape
    tm = bsz * tb
    _from_chunk_rows(yp_ref, ypre_ref, bsz, tb)
    u = su_ref[...].reshape(tm, SSM_WIDTH).astype(F32)
    ypre = jnp.concatenate([ypre_ref[c] for c in range(SSM_WIDTH // LANES)], axis=1)
    y = jax.nn.gelu(ypre + d_ref[...] * u)
    z = jnp.dot(y.astype(BF16), gw_ref[...], preferred_element_type=F32) + gb_ref[...]
    sg = sg_ref[...].reshape(tm, SSM_WIDTH).astype(F32)
    y_ssm = z[:, :SSM_WIDTH] * jax.nn.sigmoid(z[:, SSM_WIDTH:]) * (sg * jax.nn.sigmoid(sg))
    acc = jnp.dot(y_ssm.astype(BF16), wo_ref[0:SSM_WIDTH, :], preferred_element_type=F32)
    acc = acc + jnp.dot(ypool_ref[...].reshape(tm, POOL_WIDTH), wo_ref[SSM_WIDTH:SSM_WIDTH + POOL_WIDTH, :],
                        preferred_element_type=F32)
    acc = acc + jnp.dot(yattn_ref[...].reshape(tm, ATTN_WIDTH), wo_ref[SSM_WIDTH + POOL_WIDTH:, :],
                        preferred_element_type=F32)
    ms = jnp.mean(acc * acc, axis=-1, keepdims=True)
    y_out = acc * lax.rsqrt(ms + RMS_EPS) * pg_ref[...]
    o_ref[...] = x_ref[...] + y_out.reshape(bsz, tb, D_MODEL)


def _out_proj(x, yp, proj, ypool, yattn, d, glu_w, glu_b, w_out, post_g, tb=TOKENS_PER_TILE):
    bsz, seq, _ = x.shape
    tm = bsz * tb
    tok = lambda c: (lambda i: (0, i, c))
    const = lambda i: (0, 0)
    return pl.pallas_call(
        _out_kernel,
        grid=(seq // tb,),
        in_specs=[pl.BlockSpec((bsz, tb, D_MODEL), tok(0)),
                  pl.BlockSpec((SSM_PAIRS, tm // SSM_CHUNK, PAIR_WIDTH), lambda i: (0, i, 0)),
                  pl.BlockSpec((bsz, tb, SSM_WIDTH), tok(COL_SU // SSM_WIDTH)),
                  pl.BlockSpec((bsz, tb, SSM_WIDTH), tok(COL_SG // SSM_WIDTH)),
                  pl.BlockSpec((bsz, tb, POOL_WIDTH), tok(0)),
                  pl.BlockSpec((bsz, tb, ATTN_WIDTH), tok(0)),
                  pl.BlockSpec((1, SSM_WIDTH), const),
                  pl.BlockSpec((SSM_WIDTH, 2 * SSM_WIDTH), const),
                  pl.BlockSpec((1, 2 * SSM_WIDTH), const),
                  pl.BlockSpec((D_MODEL, D_MODEL), const),
                  pl.BlockSpec((1, D_MODEL), const)],
        out_specs=pl.BlockSpec((bsz, tb, D_MODEL), tok(0)),
        out_shape=jax.ShapeDtypeStruct((bsz, seq, D_MODEL), F32),
        scratch_shapes=[pltpu.VMEM((SSM_WIDTH // LANES, tm, LANES), F32)],
        compiler_params=pltpu.CompilerParams(dimension_semantics=("arbitrary",),
                                             vmem_limit_bytes=48 * 1024 * 1024),
        name="out_proj",
    )(x, yp, proj, proj, ypool, yattn, d, glu_w, glu_b, w_out, post_g)


def kernel(x, pre_norm_g, w_in, ssm_a_re, ssm_a_im, ssm_log_dt, ssm_b_re, ssm_b_im, ssm_c_re, ssm_c_im,
           ssm_d, ssm_glu_w, ssm_glu_b, pool_w, pool_scale, attn_sink, w_out, post_norm_g):
    bsz, seq, _ = x.shape
    depth = w_in.shape[0]
    assert seq % BLOCK == 0 and seq % SSM_CHUNK == 0 and bsz == SUBLANES

    col_scale = np.ones((IN_WIDTH,), np.float32)
    col_scale[COL_Q:COL_Q + ATTN_WIDTH] = HEAD_DIM ** -0.5 * LOG2_E
    w_in_b = (w_in[:, :, _in_col_permutation()] * col_scale).astype(BF16)
    w_out_b = w_out.astype(BF16)
    glu_w_b = ssm_glu_w.astype(BF16)
    eye = jnp.eye(len(POOL_WINDOWS), dtype=F32)
    pool_bd = jnp.einsum('lgcd,gh->lgchd', pool_w.astype(F32), eye).reshape(
        depth, POOL_WIDTH, POOL_WIDTH).astype(BF16)
    w_t, w_sf, w_sb, w_hf, w_hb, lam = _ssm_weights(
        ssm_a_re.astype(F32), ssm_a_im.astype(F32), ssm_log_dt.astype(F32), ssm_b_re.astype(F32),
        ssm_b_im.astype(F32), ssm_c_re.astype(F32), ssm_c_im.astype(F32))
    bias = jnp.asarray(_attn_bias())

    for l in range(depth):
        proj, zp = _inproj(x, pre_norm_g[l].reshape(1, D_MODEL), w_in_b[l])
        hf, hb = _ssm_scan(zp, w_sf[l], w_sb[l], lam[l])
        yp = _ssm_out(zp, hf, hb, w_t[l], w_hf[l], w_hb[l])
        proj2d = proj.reshape(bsz * seq, IN_WIDTH)
        ypool = _pool(proj2d, pool_bd[l], pool_scale[l].reshape(1, POOL_WIDTH), bsz, seq)
        yattn = _attention(proj2d, attn_sink[l].astype(F32) * LOG2_E, bias, bsz, seq)
        x = _out_proj(x, yp, proj, ypool.reshape(bsz, seq, POOL_WIDTH), yattn.reshape(bsz, seq, ATTN_WIDTH),
                      ssm_d[l].reshape(1, SSM_WIDTH), glu_w_b[l], ssm_glu_b[l].reshape(1, 2 * SSM_WIDTH),
                      w_out_b[l], post_norm_g[l].reshape(1, D_MODEL))
    return x
```

```python
import functools
import math

import numpy as np
import jax
import jax.numpy as jnp
from jax import lax
from jax.experimental import pallas as pl
from jax.experimental.pallas import tpu as pltpu

F32 = jnp.float32
BF16 = jnp.bfloat16
HIGHEST = lax.Precision.HIGHEST

D_MODEL = 1024
SSM_WIDTH = 256
POOL_WIDTH = 256
ATTN_WIDTH = 512
SSM_GROUP = 16
SSM_GROUPS = SSM_WIDTH // SSM_GROUP
SSM_PAIRS = SSM_GROUPS // 2
SSM_STATE = 64
POOL_WINDOWS = (2, 4, 8, 16)
POOL_GROUP = POOL_WIDTH // len(POOL_WINDOWS)
POOL_HALO = 8
HEAD_DIM = 64
N_HEADS = ATTN_WIDTH // HEAD_DIM
N_KV_HEADS = 2
KV_WIDTH = N_KV_HEADS * HEAD_DIM
Q_PER_KV = N_HEADS // N_KV_HEADS
ATTN_Q_BLOCKS = 4
WINDOW = 128
BLOCK = 128
RMS_EPS = 1e-6
NEG_BIG = -1e30
LOG2_E = math.log2(math.e)

SSM_CHUNK = 16
CHUNK_WIDTH = SSM_CHUNK * SSM_GROUP
PAIR_WIDTH = 2 * CHUNK_WIDTH
SUBLANES = 8
TOKENS_PER_TILE = 64
LANES = 128
PIECE = 2 * SSM_GROUP

COL_SU, COL_SG, COL_PU, COL_PG = 0, 256, 512, 768
COL_Q, COL_AG, COL_K, COL_V = 1024, 1536, 2048, 2176
IN_WIDTH = COL_V + KV_WIDTH


def _permute_in_cols(w):
    ref_k, ref_v, ref_ag = 1536, 1664, 1792
    return jnp.concatenate([w[..., :ref_k], w[..., ref_ag:ref_ag + ATTN_WIDTH],
                            w[..., ref_k:ref_k + KV_WIDTH], w[..., ref_v:ref_v + KV_WIDTH]], axis=-1)


def _lane_group_masks(rows):
    lane = lax.broadcasted_iota(jnp.int32, (rows, LANES), 1)
    return [(lane >= k * PIECE) & (lane < (k + 1) * PIECE) for k in range(LANES // PIECE)]


def _to_chunk_rows(su_ref, z_ref, bsz, tb):
    cpt = tb // SSM_CHUNK
    masks = _lane_group_masks(cpt * bsz)
    per_vreg = LANES // PIECE
    rows = [[jnp.concatenate([su_ref[col, pl.ds(c * SSM_CHUNK + i, bsz, stride=tb), :] for c in range(cpt)],
                             axis=0) for col in range(SSM_WIDTH // LANES)]
            for i in range(SSM_CHUNK)]
    for m in range(SSM_PAIRS):
        col, src_off = divmod(m * PIECE, LANES)
        for t in range(PAIR_WIDTH // LANES):
            acc = None
            for k in range(per_vreg):
                r = pltpu.roll(rows[t * per_vreg + k][col], (k * PIECE - src_off) % LANES, axis=1)
                acc = r if acc is None else jnp.where(masks[k], r, acc)
            z_ref[m, :, t * LANES:(t + 1) * LANES] = acc.astype(BF16)


def _from_chunk_rows(y_ref, out_ref, bsz, tb):
    cpt = tb // SSM_CHUNK
    masks = _lane_group_masks(cpt * bsz)
    per_vreg = LANES // PIECE
    for t in range(PAIR_WIDTH // LANES):
        src = [y_ref[m, :, t * LANES:(t + 1) * LANES].astype(F32) for m in range(SSM_PAIRS)]
        for k in range(per_vreg):
            j = t * per_vreg + k
            for col in range(SSM_WIDTH // LANES):
                acc = None
                for mm in range(per_vreg):
                    r = pltpu.roll(src[col * per_vreg + mm], (mm * PIECE - k * PIECE) % LANES, axis=1)
                    acc = r if acc is None else jnp.where(masks[mm], r, acc)
                for c in range(cpt):
                    out_ref[col, pl.ds(c * SSM_CHUNK + j, bsz, stride=tb), :] = acc[c * bsz:(c + 1) * bsz]


def _inproj_kernel(x_ref, g_ref, w_ref, o_ref, z_ref, h_ref, su_ref):
    bsz, tb, _ = x_ref.shape
    x = x_ref[...].reshape(bsz * tb, D_MODEL)
    ms = jnp.mean(x * x, axis=-1, keepdims=True)
    h_ref[...] = (x * lax.rsqrt(ms + RMS_EPS) * g_ref[...]).astype(BF16)
    n_tile = 256
    for j in range(IN_WIDTH // n_tile):
        sl = slice(j * n_tile, (j + 1) * n_tile)
        r = jnp.dot(h_ref[...], w_ref[:, sl], preferred_element_type=F32)
        o_ref[:, :, sl] = r.astype(BF16).reshape(bsz, tb, n_tile)
        if j * n_tile == COL_SU:
            for c in range(SSM_WIDTH // LANES):
                su_ref[c] = r[:, c * LANES:(c + 1) * LANES]
    _to_chunk_rows(su_ref, z_ref, bsz, tb)


def _inproj(x, g, w, tb=TOKENS_PER_TILE):
    bsz, seq, _ = x.shape
    tm = bsz * tb
    rows_per_tile = tm // SSM_CHUNK
    return pl.pallas_call(
        _inproj_kernel,
        grid=(seq // tb,),
        in_specs=[pl.BlockSpec((bsz, tb, D_MODEL), lambda i: (0, i, 0)),
                  pl.BlockSpec((1, D_MODEL), lambda i: (0, 0)),
                  pl.BlockSpec((D_MODEL, IN_WIDTH), lambda i: (0, 0))],
        out_specs=[pl.BlockSpec((bsz, tb, IN_WIDTH), lambda i: (0, i, 0)),
                   pl.BlockSpec((SSM_PAIRS, rows_per_tile, PAIR_WIDTH), lambda i: (0, i, 0))],
        out_shape=[jax.ShapeDtypeStruct((bsz, seq, IN_WIDTH), BF16),
                   jax.ShapeDtypeStruct((SSM_PAIRS, bsz * seq // SSM_CHUNK, PAIR_WIDTH), BF16)],
        scratch_shapes=[pltpu.VMEM((tm, D_MODEL), BF16),
                        pltpu.VMEM((SSM_WIDTH // LANES, tm, LANES), F32)],
        compiler_params=pltpu.CompilerParams(dimension_semantics=("arbitrary",),
                                             vmem_limit_bytes=48 * 1024 * 1024),
        name="inproj",
    )(x, g, w)


def _ssm_weights(a_re, a_im, log_dt, b_re, b_im, c_re, c_im):
    depth = a_re.shape[0]
    T = SSM_CHUNK
    dt = jnp.exp(log_dt)[..., None]
    ar, ai = a_re * dt, a_im * dt
    mag = jnp.exp(ar)
    lb_re, lb_im = mag * jnp.cos(ai), mag * jnp.sin(ai)
    den = a_re * a_re + a_im * a_im
    num_re = lb_re - 1.0
    coef_re = (num_re * a_re + lb_im * a_im) / den
    coef_im = (lb_im * a_re - num_re * a_im) / den
    bb_re = coef_re[..., None] * b_re - coef_im[..., None] * b_im
    bb_im = coef_re[..., None] * b_im + coef_im[..., None] * b_re
    tau = jnp.arange(T + 1, dtype=F32)[:, None, None]
    pw_mag = jnp.exp(tau * ar[:, :, None])
    pw_re = pw_mag * jnp.cos(tau * ai[:, :, None])
    pw_im = pw_mag * jnp.sin(tau * ai[:, :, None])
    half = 2 * SSM_STATE
    pwp_re = pw_re.reshape(depth, 2, T + 1, SSM_PAIRS, half)
    pwp_im = pw_im.reshape(depth, 2, T + 1, SSM_PAIRS, half)
    eye2 = jnp.eye(2, dtype=F32)

    def pair_blocks(a):
        r = a.shape[3]
        a = a.reshape(depth, 2, SSM_PAIRS, 2, r, 1, SSM_STATE)
        return (a * eye2[:, None, :, None]).reshape(depth, 2, SSM_PAIRS, 2 * r, half)

    bp_re = pair_blocks(jnp.transpose(bb_re, (0, 1, 2, 4, 3)))
    bp_im = pair_blocks(jnp.transpose(bb_im, (0, 1, 2, 4, 3)))
    cp_re = pair_blocks(c_re)
    cp_im = pair_blocks(c_im)

    def scaled(direction, expo, x_re, x_im):
        pr = jnp.transpose(pwp_re[:, direction][:, expo], (0, 2, 1, 3))[:, :, :, None]
        pi = jnp.transpose(pwp_im[:, direction][:, expo], (0, 2, 1, 3))[:, :, :, None]
        xr, xi = x_re[:, direction][:, :, None], x_im[:, direction][:, :, None]
        return pr * xr - pi * xi, pr * xi + pi * xr

    def rows_by_step(re, im):
        return jnp.concatenate([re, im], axis=-1).reshape(depth, SSM_PAIRS, PAIR_WIDTH, 2 * half)

    w_sf = rows_by_step(*scaled(0, jnp.arange(T - 1, -1, -1), bp_re, bp_im))
    w_sb = rows_by_step(*scaled(1, jnp.arange(T), bp_re, bp_im))

    def state_out(direction, expo):
        m_re, m_im = scaled(direction, expo, cp_re, cp_im)
        return rows_by_step(m_re, -m_im)

    w_hf = state_out(0, jnp.arange(1, T + 1))
    w_hb = state_out(1, jnp.arange(T, 0, -1))

    def lag_kernels(direction):
        s_re, s_im = scaled(direction, jnp.arange(T), bp_re, bp_im)
        return (jnp.einsum('dmtxk,dmyk->dmxty', s_re, cp_re[:, direction], precision=HIGHEST)
                - jnp.einsum('dmtxk,dmyk->dmxty', s_im, cp_im[:, direction], precision=HIGHEST))

    kf, kb = lag_kernels(0), lag_kernels(1)
    by_lag = jnp.concatenate([jnp.flip(kb[:, :, :, 1:], axis=3), kf[:, :, :, :1] + kb[:, :, :, :1],
                              kf[:, :, :, 1:]], axis=3).reshape(depth, SSM_PAIRS, PIECE, (2 * T - 1) * PIECE)
    w_t = jnp.stack([by_lag[..., (T - 1 - i) * PIECE:(T - 1 - i) * PIECE + PAIR_WIDTH] for i in range(T)],
                    axis=2).reshape(depth, SSM_PAIRS, PAIR_WIDTH, PAIR_WIDTH)

    def lam_rows(p):
        return jnp.broadcast_to(p[:, :, None], (depth, SSM_PAIRS, SUBLANES, half))

    lam = jnp.stack([lam_rows(pwp_re[:, 0, T]), lam_rows(pwp_im[:, 0, T]),
                     lam_rows(pwp_re[:, 1, T]), lam_rows(pwp_im[:, 1, T])], axis=1)
    return (w_t.astype(BF16), w_sf.astype(BF16), w_sb.astype(BF16),
            w_hf.astype(BF16), w_hb.astype(BF16), lam)


def _ssm_scan_kernel(zf_ref, zb_ref, wsf_ref, wsb_ref, lam_ref, hf_out, hb_out,
                     sfr, sfi, sbr, sbi, carry):
    rc = zf_ref.shape[1]
    half = 2 * SSM_STATE

    @pl.when(pl.program_id(0) == 0)
    def _():
        carry[...] = jnp.zeros_like(carry)

    for m in range(SSM_PAIRS):
        s = jnp.dot(zf_ref[m], wsf_ref[m], preferred_element_type=F32)
        sfr[m] = s[:, :half]
        sfi[m] = s[:, half:]
        s = jnp.dot(zb_ref[m], wsb_ref[m], preferred_element_type=F32)
        sbr[m] = s[:, :half]
        sbi[m] = s[:, half:]

    lfr, lfi, lbr, lbi = lam_ref[0], lam_ref[1], lam_ref[2], lam_ref[3]
    n_steps = rc // SUBLANES

    def body(k, c):
        hfr, hfi, hbr, hbi = c
        r = pl.multiple_of(k * SUBLANES, SUBLANES)
        hf_out[:, pl.ds(r, SUBLANES), :half] = hfr
        hf_out[:, pl.ds(r, SUBLANES), half:] = hfi
        nfr = lfr * hfr - lfi * hfi + sfr[:, pl.ds(r, SUBLANES), :]
        nfi = lfr * hfi + lfi * hfr + sfi[:, pl.ds(r, SUBLANES), :]
        rb = pl.multiple_of(rc - SUBLANES - k * SUBLANES, SUBLANES)
        hb_out[:, pl.ds(rb, SUBLANES), :half] = hbr
        hb_out[:, pl.ds(rb, SUBLANES), half:] = hbi
        nbr = lbr * hbr - lbi * hbi + sbr[:, pl.ds(rb, SUBLANES), :]
        nbi = lbr * hbi + lbi * hbr + sbi[:, pl.ds(rb, SUBLANES), :]
        return nfr, nfi, nbr, nbi

    c = lax.fori_loop(0, n_steps, body, (carry[0], carry[1], carry[2], carry[3]))
    carry[0], carry[1], carry[2], carry[3] = c


def _ssm_scan(zp, w_sf, w_sb, lam, rc=256):
    n_rows = zp.shape[1]
    n_tiles = n_rows // rc
    half = 2 * SSM_STATE
    out = jax.ShapeDtypeStruct((SSM_PAIRS, n_rows, 2 * half), F32)
    s_scratch = pltpu.VMEM((SSM_PAIRS, rc, half), F32)
    return pl.pallas_call(
        _ssm_scan_kernel,
        grid=(n_tiles,),
        in_specs=[pl.BlockSpec((SSM_PAIRS, rc, PAIR_WIDTH), lambda i: (0, i, 0)),
                  pl.BlockSpec((SSM_PAIRS, rc, PAIR_WIDTH), lambda i: (0, n_tiles - 1 - i, 0)),
                  pl.BlockSpec((SSM_PAIRS, PAIR_WIDTH, 2 * half), lambda i: (0, 0, 0)),
                  pl.BlockSpec((SSM_PAIRS, PAIR_WIDTH, 2 * half), lambda i: (0, 0, 0)),
                  pl.BlockSpec((4, SSM_PAIRS, SUBLANES, half), lambda i: (0, 0, 0, 0))],
        out_specs=[pl.BlockSpec((SSM_PAIRS, rc, 2 * half), lambda i: (0, i, 0)),
                   pl.BlockSpec((SSM_PAIRS, rc, 2 * half), lambda i: (0, n_tiles - 1 - i, 0))],
        out_shape=[out, out],
        scratch_shapes=[s_scratch, s_scratch, s_scratch, s_scratch,
                        pltpu.VMEM((4, SSM_PAIRS, SUBLANES, half), F32)],
        compiler_params=pltpu.CompilerParams(dimension_semantics=("arbitrary",),
                                             vmem_limit_bytes=48 * 1024 * 1024),
        name="ssm_scan",
    )(zp, zp, w_sf, w_sb, lam)


def _ssm_out_kernel(z_ref, hf_ref, hb_ref, wt_ref, whf_ref, whb_ref, y_ref):
    nt = (((1,), (1,)), ((), ()))
    y = jnp.dot(z_ref[...], wt_ref[...], preferred_element_type=F32)
    y = y + lax.dot_general(hf_ref[...].astype(BF16), whf_ref[...], nt, preferred_element_type=F32)
    y = y + lax.dot_general(hb_ref[...].astype(BF16), whb_ref[...], nt, preferred_element_type=F32)
    y_ref[...] = y.astype(BF16)


def _ssm_out(zp, hf, hb, w_t, w_hf, w_hb, rt=512):
    n_rows = zp.shape[1]
    rt = min(rt, n_rows)
    half2 = 4 * SSM_STATE
    return pl.pallas_call(
        _ssm_out_kernel,
        grid=(SSM_PAIRS, n_rows // rt),
        in_specs=[pl.BlockSpec((None, rt, PAIR_WIDTH), lambda m, i: (m, i, 0)),
                  pl.BlockSpec((None, rt, half2), lambda m, i: (m, i, 0)),
                  pl.BlockSpec((None, rt, half2), lambda m, i: (m, i, 0)),
                  pl.BlockSpec((None, PAIR_WIDTH, PAIR_WIDTH), lambda m, i: (m, 0, 0)),
                  pl.BlockSpec((None, PAIR_WIDTH, half2), lambda m, i: (m, 0, 0)),
                  pl.BlockSpec((None, PAIR_WIDTH, half2), lambda m, i: (m, 0, 0))],
        out_specs=pl.BlockSpec((None, rt, PAIR_WIDTH), lambda m, i: (m, i, 0)),
        out_shape=jax.ShapeDtypeStruct((SSM_PAIRS, n_rows, PAIR_WIDTH), BF16),
        compiler_params=pltpu.CompilerParams(dimension_semantics=("arbitrary", "arbitrary")),
        name="ssm_out",
    )(zp, hf, hb, w_t, w_hf, w_hb)


def _pool_kernel(pu_ref, pg_ref, w_ref, scale_ref, o_ref, pad_ref, *, seq, rows):
    halo = POOL_HALO
    zeros = jnp.zeros((halo, POOL_WIDTH), F32)
    pad_ref[0:halo, :] = zeros
    pad_ref[seq + halo:seq + 2 * halo, :] = zeros
    pad_ref[halo:seq + halo, :] = pu_ref[...].astype(F32)

    lane = lax.broadcasted_iota(jnp.int32, (rows, POOL_WIDTH), 1)
    half_win = jnp.where(lane < POOL_GROUP, 1,
                         jnp.where(lane < 2 * POOL_GROUP, 2,
                                   jnp.where(lane < 3 * POOL_GROUP, 4, 8)))
    row = lax.broadcasted_iota(jnp.int32, (rows, POOL_WIDTH), 0)

    ext = rows + 2 * halo

    def shift(a, d):
        return pltpu.roll(a, (-d) % ext, axis=0)

    def body(k, _):
        base = pl.multiple_of(k * rows, rows)
        xs = pad_ref[pl.ds(base, ext), :]
        w2 = shift(xs, -1) + xs
        w4 = shift(w2, -1) + shift(w2, 1)
        w8 = shift(w4, -2) + shift(w4, 2)
        w16 = shift(w8, -4) + shift(w8, 4)
        centre = xs[halo:halo + rows]
        total = jnp.where(half_win == 1, w2[halo:halo + rows],
                          jnp.where(half_win == 2, w4[halo:halo + rows],
                                    jnp.where(half_win == 4, w8[halo:halo + rows],
                                              w16[halo:halo + rows])))
        t = row + base
        count = jnp.minimum(t + half_win, seq) - jnp.maximum(t - half_win, 0)
        diff = total / count.astype(F32) - centre
        y = jnp.dot(diff.astype(BF16), w_ref[...], preferred_element_type=F32)
        g = pg_ref[pl.ds(base, rows), :].astype(F32)
        o_ref[pl.ds(base, rows), :] = (y * scale_ref[...] * (g * jax.nn.sigmoid(g))).astype(BF16)
        return 0

    lax.fori_loop(0, seq // rows, body, 0)


def _pool(proj, w_bd, scale, bsz, seq, rows=256):
    kern = functools.partial(_pool_kernel, seq=seq, rows=rows)
    return pl.pallas_call(
        kern,
        grid=(bsz,),
        in_specs=[pl.BlockSpec((seq, POOL_WIDTH), lambda b: (b, COL_PU // POOL_WIDTH)),
                  pl.BlockSpec((seq, POOL_WIDTH), lambda b: (b, COL_PG // POOL_WIDTH)),
                  pl.BlockSpec((POOL_WIDTH, POOL_WIDTH), lambda b: (0, 0)),
                  pl.BlockSpec((1, POOL_WIDTH), lambda b: (0, 0))],
        out_specs=pl.BlockSpec((seq, POOL_WIDTH), lambda b: (b, 0)),
        out_shape=jax.ShapeDtypeStruct((bsz * seq, POOL_WIDTH), BF16),
        scratch_shapes=[pltpu.VMEM((seq + 2 * POOL_HALO, POOL_WIDTH), F32)],
        compiler_params=pltpu.CompilerParams(dimension_semantics=("arbitrary",)),
        name="pool",
    )(proj, proj, w_bd, scale)


def _attn_bias():
    slopes = np.exp2(-8.0 * np.arange(1, N_HEADS + 1, dtype=np.float32) / N_HEADS).astype(np.float32)
    qpos = np.arange(BLOCK)[:, None]
    kpos = np.arange(3 * BLOCK)[None, :] - BLOCK
    dist = np.abs(qpos - kpos)
    bias = -slopes[:, None, None] * dist.astype(np.float32)[None]
    bias = np.where(dist[None] <= WINDOW, bias * np.float32(LOG2_E), np.float32(NEG_BIG)).astype(np.float32)
    first, last = bias.copy(), bias.copy()
    first[:, :, :BLOCK] = NEG_BIG
    last[:, :, 2 * BLOCK:] = NEG_BIG
    return np.stack([first, bias, last])


def _attn_kernel(sink_ref, q_ref, kp_ref, kc_ref, kn_ref, vp_ref, vc_ref, vn_ref, ag_ref, bias_ref,
                 o_ref, s_ref, p_ref, *, n_blocks, qb):
    n = pl.program_id(1)
    nt = (((1,), (1,)), ((), ()))
    band = 3 * BLOCK
    keys = (qb + 2) * BLOCK
    lo = lax.broadcasted_iota(jnp.int32, (keys, KV_WIDTH), 1) < HEAD_DIM

    def per_head_halves(a_ref3):
        a = jnp.concatenate([r[...] for r in a_ref3], axis=0).astype(F32)
        swapped = pltpu.roll(a, HEAD_DIM, axis=1)
        return [[jnp.where(lo, a, 0.0).astype(BF16), jnp.where(lo, 0.0, swapped).astype(BF16)],
                [jnp.where(lo, swapped, 0.0).astype(BF16), jnp.where(lo, 0.0, a).astype(BF16)]]

    k_sel = per_head_halves((kp_ref, kc_ref, kn_ref))
    v_sel = per_head_halves((vp_ref, vc_ref, vn_ref))
    lo_q = lax.broadcasted_iota(jnp.int32, (BLOCK, 2 * HEAD_DIM), 1) < HEAD_DIM

    for j in range(qb):
        blk = n * qb + j
        variant = jnp.where(blk == 0, 0, jnp.where(blk == n_blocks - 1, 2, 1))
        q_rows = slice(j * BLOCK, (j + 1) * BLOCK)
        k_rows = slice(j * BLOCK, j * BLOCK + band)

        for h in range(N_HEADS):
            pair, odd = divmod(h, 2)
            qp = q_ref[q_rows, pair * 2 * HEAD_DIM:(pair + 1) * 2 * HEAD_DIM]
            s_ref[j, h] = (lax.dot_general(qp, k_sel[h // Q_PER_KV][odd][k_rows], nt,
                                           preferred_element_type=F32) + bias_ref[variant, h])

        inv_l = []
        for h in range(N_HEADS):
            s = s_ref[j, h]
            sink = sink_ref[h]
            m = jnp.maximum(jnp.max(s, axis=-1, keepdims=True), sink)
            p = jnp.exp2(s - m)
            inv_l.append(1.0 / (jnp.sum(p, axis=-1, keepdims=True) + jnp.exp2(sink - m)))
            p_ref[j, h] = p.astype(BF16)

        for pair in range(N_HEADS // 2):
            kv = (2 * pair) // Q_PER_KV
            o = (jnp.dot(p_ref[j, 2 * pair], v_sel[kv][0][k_rows], preferred_element_type=F32)
                 + jnp.dot(p_ref[j, 2 * pair + 1], v_sel[kv][1][k_rows], preferred_element_type=F32))
            o = o * jnp.where(lo_q, inv_l[2 * pair], inv_l[2 * pair + 1])
            sl = slice(pair * 2 * HEAD_DIM, (pair + 1) * 2 * HEAD_DIM)
            g = ag_ref[q_rows, sl].astype(F32)
            o_ref[q_rows, sl] = (o * (g * jax.nn.sigmoid(g))).astype(BF16)


def _attention(proj, sink, bias, bsz, seq, qb=ATTN_Q_BLOCKS):
    nb = seq // BLOCK
    assert nb >= 2 and nb % qb == 0
    steps = nb // qb
    kern = functools.partial(_attn_kernel, n_blocks=nb, qb=qb)
    kcol, vcol = COL_K // KV_WIDTH, COL_V // KV_WIDTH
    tq = qb * BLOCK

    def own(width, col):
        return pl.BlockSpec((tq, width), lambda b, n: (b * steps + n, col))

    def neighbour(col, prev):
        def index(b, n):
            blk = jnp.maximum(n * qb - 1, 0) if prev else jnp.minimum((n + 1) * qb, nb - 1)
            return (b * nb + blk, col)
        return pl.BlockSpec((BLOCK, KV_WIDTH), index)

    return pl.pallas_call(
        kern,
        grid=(bsz, steps),
        in_specs=[pl.BlockSpec(memory_space=pltpu.SMEM),
                  own(ATTN_WIDTH, COL_Q // ATTN_WIDTH),
                  neighbour(kcol, True), own(KV_WIDTH, kcol), neighbour(kcol, False),
                  neighbour(vcol, True), own(KV_WIDTH, vcol), neighbour(vcol, False),
                  own(ATTN_WIDTH, COL_AG // ATTN_WIDTH),
                  pl.BlockSpec((3, N_HEADS, BLOCK, 3 * BLOCK), lambda b, n: (0, 0, 0, 0))],
        out_specs=pl.BlockSpec((tq, ATTN_WIDTH), lambda b, n: (b * steps + n, 0)),
        out_shape=jax.ShapeDtypeStruct((bsz * seq, ATTN_WIDTH), BF16),
        scratch_shapes=[pltpu.VMEM((qb, N_HEADS, BLOCK, 3 * BLOCK), F32),
                        pltpu.VMEM((qb, N_HEADS, BLOCK, 3 * BLOCK), BF16)],
        compiler_params=pltpu.CompilerParams(dimension_semantics=("arbitrary", "arbitrary"),
                                             vmem_limit_bytes=48 * 1024 * 1024),
        name="attention",
    )(sink, proj, proj, proj, proj, proj, proj, proj, proj, bias)


def _out_kernel(x_ref, yp_ref, su_ref, sg_ref, ypool_ref, yattn_ref, d_ref, gw_ref, gb_ref,
                wo_ref, pg_ref, o_ref, ypre_ref):
    bsz, tb, _ = x_ref.shape
    tm = bsz * tb
    _from_chunk_rows(yp_ref, ypre_ref, bsz, tb)
    u = su_ref[...].reshape(tm, SSM_WIDTH).astype(F32)
    ypre = jnp.concatenate([ypre_ref[c] for c in range(SSM_WIDTH // LANES)], axis=1)
    y = jax.nn.gelu(ypre + d_ref[...] * u)
    z = jnp.dot(y.astype(BF16), gw_ref[...], preferred_element_type=F32) + gb_ref[...]
    sg = sg_ref[...].reshape(tm, SSM_WIDTH).astype(F32)
    y_ssm = z[:, :SSM_WIDTH] * jax.nn.sigmoid(z[:, SSM_WIDTH:]) * (sg * jax.nn.sigmoid(sg))
    acc = jnp.dot(y_ssm.astype(BF16), wo_ref[0:SSM_WIDTH, :], preferred_element_type=F32)
    acc = acc + jnp.dot(ypool_ref[...].reshape(tm, POOL_WIDTH), wo_ref[SSM_WIDTH:SSM_WIDTH + POOL_WIDTH, :],
                        preferred_element_type=F32)
    acc = acc + jnp.dot(yattn_ref[...].reshape(tm, ATTN_WIDTH), wo_ref[SSM_WIDTH + POOL_WIDTH:, :],
                        preferred_element_type=F32)
    ms = jnp.mean(acc * acc, axis=-1, keepdims=True)
    y_out = acc * lax.rsqrt(ms + RMS_EPS) * pg_ref[...]
    o_ref[...] = x_ref[...] + y_out.reshape(bsz, tb, D_MODEL)


def _out_proj(x, yp, proj, ypool, yattn, d, glu_w, glu_b, w_out, post_g, tb=TOKENS_PER_TILE):
    bsz, seq, _ = x.shape
    tm = bsz * tb
    tok = lambda c: (lambda i: (0, i, c))
    const = lambda i: (0, 0)
    return pl.pallas_call(
        _out_kernel,
        grid=(seq // tb,),
        in_specs=[pl.BlockSpec((bsz, tb, D_MODEL), tok(0)),
                  pl.BlockSpec((SSM_PAIRS, tm // SSM_CHUNK, PAIR_WIDTH), lambda i: (0, i, 0)),
                  pl.BlockSpec((bsz, tb, SSM_WIDTH), tok(COL_SU // SSM_WIDTH)),
                  pl.BlockSpec((bsz, tb, SSM_WIDTH), tok(COL_SG // SSM_WIDTH)),
                  pl.BlockSpec((bsz, tb, POOL_WIDTH), tok(0)),
                  pl.BlockSpec((bsz, tb, ATTN_WIDTH), tok(0)),
                  pl.BlockSpec((1, SSM_WIDTH), const),
                  pl.BlockSpec((SSM_WIDTH, 2 * SSM_WIDTH), const),
                  pl.BlockSpec((1, 2 * SSM_WIDTH), const),
                  pl.BlockSpec((D_MODEL, D_MODEL), const),
                  pl.BlockSpec((1, D_MODEL), const)],
        out_specs=pl.BlockSpec((bsz, tb, D_MODEL), tok(0)),
        out_shape=jax.ShapeDtypeStruct((bsz, seq, D_MODEL), F32),
        scratch_shapes=[pltpu.VMEM((SSM_WIDTH // LANES, tm, LANES), F32)],
        compiler_params=pltpu.CompilerParams(dimension_semantics=("arbitrary",),
                                             vmem_limit_bytes=48 * 1024 * 1024),
        name="out_proj",
    )(x, yp, proj, proj, ypool, yattn, d, glu_w, glu_b, w_out, post_g)


def kernel(x, pre_norm_g, w_in, ssm_a_re, ssm_a_im, ssm_log_dt, ssm_b_re, ssm_b_im, ssm_c_re, ssm_c_im,
           ssm_d, ssm_glu_w, ssm_glu_b, pool_w, pool_scale, attn_sink, w_out, post_norm_g):
    bsz, seq, _ = x.shape
    depth = w_in.shape[0]
    assert seq % BLOCK == 0 and seq % SSM_CHUNK == 0 and bsz == SUBLANES

    col_scale = np.ones((IN_WIDTH,), np.float32)
    col_scale[COL_Q:COL_Q + ATTN_WIDTH] = HEAD_DIM ** -0.5 * LOG2_E
    w_in_b = (_permute_in_cols(w_in) * col_scale).astype(BF16)
    w_out_b = w_out.astype(BF16)
    glu_w_b = ssm_glu_w.astype(BF16)
    eye = jnp.eye(len(POOL_WINDOWS), dtype=F32)
    pool_bd = jnp.einsum('lgcd,gh->lgchd', pool_w.astype(F32), eye).reshape(
        depth, POOL_WIDTH, POOL_WIDTH).astype(BF16)
    w_t, w_sf, w_sb, w_hf, w_hb, lam = _ssm_weights(
        ssm_a_re.astype(F32), ssm_a_im.astype(F32), ssm_log_dt.astype(F32), ssm_b_re.astype(F32),
        ssm_b_im.astype(F32), ssm_c_re.astype(F32), ssm_c_im.astype(F32))
    bias = jnp.asarray(_attn_bias())

    for l in range(depth):
        proj, zp = _inproj(x, pre_norm_g[l].reshape(1, D_MODEL), w_in_b[l])
        hf, hb = _ssm_scan(zp, w_sf[l], w_sb[l], lam[l])
        yp = _ssm_out(zp, hf, hb, w_t[l], w_hf[l], w_hb[l])
        proj2d = proj.reshape(bsz * seq, IN_WIDTH)
        ypool = _pool(proj2d, pool_bd[l], pool_scale[l].reshape(1, POOL_WIDTH), bsz, seq)
        yattn = _attention(proj2d, attn_sink[l].astype(F32) * LOG2_E, bias, bsz, seq)
        x = _out_proj(x, yp, proj, ypool.reshape(bsz, seq, POOL_WIDTH), yattn.reshape(bsz, seq, ATTN_WIDTH),
                      ssm_d[l].reshape(1, SSM_WIDTH), glu_w_b[l], ssm_glu_b[l].reshape(1, 2 * SSM_WIDTH),
                      w_out_b[l], post_norm_g[l].reshape(1, D_MODEL))
    return x
```

```python
import functools
import math

import numpy as np
import jax
import jax.numpy as jnp
from jax import lax
from jax.experimental import pallas as pl
from jax.experimental.pallas import tpu as pltpu

F32 = jnp.float32
BF16 = jnp.bfloat16
HIGHEST = lax.Precision.HIGHEST

D_MODEL = 1024
SSM_WIDTH = 256
POOL_WIDTH = 256
ATTN_WIDTH = 512
SSM_GROUP = 16
SSM_GROUPS = SSM_WIDTH // SSM_GROUP
SSM_PAIRS = SSM_GROUPS // 2
SSM_STATE = 64
POOL_WINDOWS = (2, 4, 8, 16)
POOL_GROUP = POOL_WIDTH // len(POOL_WINDOWS)
POOL_HALO = 8
HEAD_DIM = 64
N_HEADS = ATTN_WIDTH // HEAD_DIM
N_KV_HEADS = 2
KV_WIDTH = N_KV_HEADS * HEAD_DIM
Q_PER_KV = N_HEADS // N_KV_HEADS
ATTN_Q_BLOCKS = 8
WINDOW = 128
BLOCK = 128
RMS_EPS = 1e-6
NEG_BIG = -1e30
LOG2_E = math.log2(math.e)

SSM_CHUNK = 16
CHUNK_WIDTH = SSM_CHUNK * SSM_GROUP
PAIR_WIDTH = 2 * CHUNK_WIDTH
SUBLANES = 8
TOKENS_PER_TILE = 128
LANES = 128
PIECE = 2 * SSM_GROUP

COL_SU, COL_SG, COL_PU, COL_PG = 0, 256, 512, 768
COL_Q, COL_AG, COL_K, COL_V = 1024, 1536, 2048, 2176
IN_WIDTH = COL_V + KV_WIDTH


def _permute_in_cols(w):
    ref_k, ref_v, ref_ag = 1536, 1664, 1792
    return jnp.concatenate([w[..., :ref_k], w[..., ref_ag:ref_ag + ATTN_WIDTH],
                            w[..., ref_k:ref_k + KV_WIDTH], w[..., ref_v:ref_v + KV_WIDTH]], axis=-1)


def _lane_group_masks(rows):
    lane = lax.broadcasted_iota(jnp.int32, (rows, LANES), 1)
    return [(lane >= k * PIECE) & (lane < (k + 1) * PIECE) for k in range(LANES // PIECE)]


def _to_chunk_rows(su_ref, z_ref, bsz, tb):
    cpt = tb // SSM_CHUNK
    masks = _lane_group_masks(cpt * bsz)
    per_vreg = LANES // PIECE
    rows = [[jnp.concatenate([su_ref[col, pl.ds(c * SSM_CHUNK + i, bsz, stride=tb), :] for c in range(cpt)],
                             axis=0) for col in range(SSM_WIDTH // LANES)]
            for i in range(SSM_CHUNK)]
    for m in range(SSM_PAIRS):
        col, src_off = divmod(m * PIECE, LANES)
        for t in range(PAIR_WIDTH // LANES):
            acc = None
            for k in range(per_vreg):
                r = pltpu.roll(rows[t * per_vreg + k][col], (k * PIECE - src_off) % LANES, axis=1)
                acc = r if acc is None else jnp.where(masks[k], r, acc)
            z_ref[m, :, t * LANES:(t + 1) * LANES] = acc.astype(BF16)


def _from_chunk_rows(y_ref, out_ref, bsz, tb):
    cpt = tb // SSM_CHUNK
    masks = _lane_group_masks(cpt * bsz)
    per_vreg = LANES // PIECE
    for t in range(PAIR_WIDTH // LANES):
        src = [y_ref[m, :, t * LANES:(t + 1) * LANES].astype(F32) for m in range(SSM_PAIRS)]
        for k in range(per_vreg):
            j = t * per_vreg + k
            for col in range(SSM_WIDTH // LANES):
                acc = None
                for mm in range(per_vreg):
                    r = pltpu.roll(src[col * per_vreg + mm], (mm * PIECE - k * PIECE) % LANES, axis=1)
                    acc = r if acc is None else jnp.where(masks[mm], r, acc)
                for c in range(cpt):
                    out_ref[col, pl.ds(c * SSM_CHUNK + j, bsz, stride=tb), :] = acc[c * bsz:(c + 1) * bsz]


def _inproj_kernel(x_ref, g_ref, w_ref, o_ref, z_ref, h_ref, su_ref):
    bsz, tb, _ = x_ref.shape
    x = x_ref[...].reshape(bsz * tb, D_MODEL)
    ms = jnp.mean(x * x, axis=-1, keepdims=True)
    h_ref[...] = (x * lax.rsqrt(ms + RMS_EPS) * g_ref[...]).astype(BF16)
    n_tile = 256
    for j in range(IN_WIDTH // n_tile):
        sl = slice(j * n_tile, (j + 1) * n_tile)
        r = jnp.dot(h_ref[...], w_ref[:, sl], preferred_element_type=F32)
        o_ref[:, :, sl] = r.astype(BF16).reshape(bsz, tb, n_tile)
        if j * n_tile == COL_SU:
            for c in range(SSM_WIDTH // LANES):
                su_ref[c] = r[:, c * LANES:(c + 1) * LANES]
    _to_chunk_rows(su_ref, z_ref, bsz, tb)


def _inproj(x, g, w, tb=TOKENS_PER_TILE):
    bsz, seq, _ = x.shape
    tm = bsz * tb
    rows_per_tile = tm // SSM_CHUNK
    return pl.pallas_call(
        _inproj_kernel,
        grid=(seq // tb,),
        in_specs=[pl.BlockSpec((bsz, tb, D_MODEL), lambda i: (0, i, 0)),
                  pl.BlockSpec((1, D_MODEL), lambda i: (0, 0)),
                  pl.BlockSpec((D_MODEL, IN_WIDTH), lambda i: (0, 0))],
        out_specs=[pl.BlockSpec((bsz, tb, IN_WIDTH), lambda i: (0, i, 0)),
                   pl.BlockSpec((SSM_PAIRS, rows_per_tile, PAIR_WIDTH), lambda i: (0, i, 0))],
        out_shape=[jax.ShapeDtypeStruct((bsz, seq, IN_WIDTH), BF16),
                   jax.ShapeDtypeStruct((SSM_PAIRS, bsz * seq // SSM_CHUNK, PAIR_WIDTH), BF16)],
        scratch_shapes=[pltpu.VMEM((tm, D_MODEL), BF16),
                        pltpu.VMEM((SSM_WIDTH // LANES, tm, LANES), F32)],
        compiler_params=pltpu.CompilerParams(dimension_semantics=("arbitrary",),
                                             vmem_limit_bytes=48 * 1024 * 1024),
        name="inproj",
    )(x, g, w)


def _ssm_weights(a_re, a_im, log_dt, b_re, b_im, c_re, c_im):
    depth = a_re.shape[0]
    T = SSM_CHUNK
    dt = jnp.exp(log_dt)[..., None]
    ar, ai = a_re * dt, a_im * dt
    mag = jnp.exp(ar)
    lb_re, lb_im = mag * jnp.cos(ai), mag * jnp.sin(ai)
    den = a_re * a_re + a_im * a_im
    num_re = lb_re - 1.0
    coef_re = (num_re * a_re + lb_im * a_im) / den
    coef_im = (lb_im * a_re - num_re * a_im) / den
    bb_re = coef_re[..., None] * b_re - coef_im[..., None] * b_im
    bb_im = coef_re[..., None] * b_im + coef_im[..., None] * b_re
    tau = jnp.arange(T + 1, dtype=F32)[:, None, None]
    pw_mag = jnp.exp(tau * ar[:, :, None])
    pw_re = pw_mag * jnp.cos(tau * ai[:, :, None])
    pw_im = pw_mag * jnp.sin(tau * ai[:, :, None])
    half = 2 * SSM_STATE
    pwp_re = pw_re.reshape(depth, 2, T + 1, SSM_PAIRS, half)
    pwp_im = pw_im.reshape(depth, 2, T + 1, SSM_PAIRS, half)
    eye2 = jnp.eye(2, dtype=F32)

    def pair_blocks(a):
        r = a.shape[3]
        a = a.reshape(depth, 2, SSM_PAIRS, 2, r, 1, SSM_STATE)
        return (a * eye2[:, None, :, None]).reshape(depth, 2, SSM_PAIRS, 2 * r, half)

    bp_re = pair_blocks(jnp.transpose(bb_re, (0, 1, 2, 4, 3)))
    bp_im = pair_blocks(jnp.transpose(bb_im, (0, 1, 2, 4, 3)))
    cp_re = pair_blocks(c_re)
    cp_im = pair_blocks(c_im)

    def scaled(direction, expo, x_re, x_im):
        pr = jnp.transpose(pwp_re[:, direction][:, expo], (0, 2, 1, 3))[:, :, :, None]
        pi = jnp.transpose(pwp_im[:, direction][:, expo], (0, 2, 1, 3))[:, :, :, None]
        xr, xi = x_re[:, direction][:, :, None], x_im[:, direction][:, :, None]
        return pr * xr - pi * xi, pr * xi + pi * xr

    def rows_by_step(re, im):
        return jnp.concatenate([re, im], axis=-1).reshape(depth, SSM_PAIRS, PAIR_WIDTH, 2 * half)

    w_sf = rows_by_step(*scaled(0, jnp.arange(T - 1, -1, -1), bp_re, bp_im))
    w_sb = rows_by_step(*scaled(1, jnp.arange(T), bp_re, bp_im))

    def state_out(direction, expo):
        m_re, m_im = scaled(direction, expo, cp_re, cp_im)
        return rows_by_step(m_re, -m_im)

    w_hf = state_out(0, jnp.arange(1, T + 1))
    w_hb = state_out(1, jnp.arange(T, 0, -1))

    def lag_kernels(direction):
        s_re, s_im = scaled(direction, jnp.arange(T), bp_re, bp_im)
        return (jnp.einsum('dmtxk,dmyk->dmxty', s_re, cp_re[:, direction], precision=HIGHEST)
                - jnp.einsum('dmtxk,dmyk->dmxty', s_im, cp_im[:, direction], precision=HIGHEST))

    kf, kb = lag_kernels(0), lag_kernels(1)
    by_lag = jnp.concatenate([jnp.flip(kb[:, :, :, 1:], axis=3), kf[:, :, :, :1] + kb[:, :, :, :1],
                              kf[:, :, :, 1:]], axis=3).reshape(depth, SSM_PAIRS, PIECE, (2 * T - 1) * PIECE)
    w_t = jnp.stack([by_lag[..., (T - 1 - i) * PIECE:(T - 1 - i) * PIECE + PAIR_WIDTH] for i in range(T)],
                    axis=2).reshape(depth, SSM_PAIRS, PAIR_WIDTH, PAIR_WIDTH)

    def lam_rows(p):
        return jnp.broadcast_to(p[:, :, None], (depth, SSM_PAIRS, SUBLANES, half))

    lam = jnp.stack([lam_rows(pwp_re[:, 0, T]), lam_rows(pwp_im[:, 0, T]),
                     lam_rows(pwp_re[:, 1, T]), lam_rows(pwp_im[:, 1, T])], axis=1)
    return (w_t.astype(BF16), w_sf.astype(BF16), w_sb.astype(BF16),
            w_hf.astype(BF16), w_hb.astype(BF16), lam)


def _ssm_scan_kernel(zf_ref, zb_ref, wsf_ref, wsb_ref, lam_ref, hf_out, hb_out,
                     sfr, sfi, sbr, sbi, carry):
    rc = zf_ref.shape[1]
    half = 2 * SSM_STATE

    @pl.when(pl.program_id(0) == 0)
    def _():
        carry[...] = jnp.zeros_like(carry)

    for m in range(SSM_PAIRS):
        s = jnp.dot(zf_ref[m], wsf_ref[m], preferred_element_type=F32)
        sfr[m] = s[:, :half]
        sfi[m] = s[:, half:]
        s = jnp.dot(zb_ref[m], wsb_ref[m], preferred_element_type=F32)
        sbr[m] = s[:, :half]
        sbi[m] = s[:, half:]

    lfr, lfi, lbr, lbi = lam_ref[0], lam_ref[1], lam_ref[2], lam_ref[3]
    n_steps = rc // SUBLANES

    def body(k, c):
        hfr, hfi, hbr, hbi = c
        r = pl.multiple_of(k * SUBLANES, SUBLANES)
        hf_out[:, pl.ds(r, SUBLANES), :half] = hfr
        hf_out[:, pl.ds(r, SUBLANES), half:] = hfi
        nfr = lfr * hfr - lfi * hfi + sfr[:, pl.ds(r, SUBLANES), :]
        nfi = lfr * hfi + lfi * hfr + sfi[:, pl.ds(r, SUBLANES), :]
        rb = pl.multiple_of(rc - SUBLANES - k * SUBLANES, SUBLANES)
        hb_out[:, pl.ds(rb, SUBLANES), :half] = hbr
        hb_out[:, pl.ds(rb, SUBLANES), half:] = hbi
        nbr = lbr * hbr - lbi * hbi + sbr[:, pl.ds(rb, SUBLANES), :]
        nbi = lbr * hbi + lbi * hbr + sbi[:, pl.ds(rb, SUBLANES), :]
        return nfr, nfi, nbr, nbi

    c = lax.fori_loop(0, n_steps, body, (carry[0], carry[1], carry[2], carry[3]))
    carry[0], carry[1], carry[2], carry[3] = c


def _ssm_scan(zp, w_sf, w_sb, lam, rc=256):
    n_rows = zp.shape[1]
    n_tiles = n_rows // rc
    half = 2 * SSM_STATE
    out = jax.ShapeDtypeStruct((SSM_PAIRS, n_rows, 2 * half), F32)
    s_scratch = pltpu.VMEM((SSM_PAIRS, rc, half), F32)
    return pl.pallas_call(
        _ssm_scan_kernel,
        grid=(n_tiles,),
        in_specs=[pl.BlockSpec((SSM_PAIRS, rc, PAIR_WIDTH), lambda i: (0, i, 0)),
                  pl.BlockSpec((SSM_PAIRS, rc, PAIR_WIDTH), lambda i: (0, n_tiles - 1 - i, 0)),
                  pl.BlockSpec((SSM_PAIRS, PAIR_WIDTH, 2 * half), lambda i: (0, 0, 0)),
                  pl.BlockSpec((SSM_PAIRS, PAIR_WIDTH, 2 * half), lambda i: (0, 0, 0)),
                  pl.BlockSpec((4, SSM_PAIRS, SUBLANES, half), lambda i: (0, 0, 0, 0))],
        out_specs=[pl.BlockSpec((SSM_PAIRS, rc, 2 * half), lambda i: (0, i, 0)),
                   pl.BlockSpec((SSM_PAIRS, rc, 2 * half), lambda i: (0, n_tiles - 1 - i, 0))],
        out_shape=[out, out],
        scratch_shapes=[s_scratch, s_scratch, s_scratch, s_scratch,
                        pltpu.VMEM((4, SSM_PAIRS, SUBLANES, half), F32)],
        compiler_params=pltpu.CompilerParams(dimension_semantics=("arbitrary",),
                                             vmem_limit_bytes=48 * 1024 * 1024),
        name="ssm_scan",
    )(zp, zp, w_sf, w_sb, lam)


def _ssm_out_kernel(z_ref, hf_ref, hb_ref, wt_ref, whf_ref, whb_ref, y_ref):
    nt = (((1,), (1,)), ((), ()))
    y = jnp.dot(z_ref[...], wt_ref[...], preferred_element_type=F32)
    y = y + lax.dot_general(hf_ref[...].astype(BF16), whf_ref[...], nt, preferred_element_type=F32)
    y = y + lax.dot_general(hb_ref[...].astype(BF16), whb_ref[...], nt, preferred_element_type=F32)
    y_ref[...] = y.astype(BF16)


def _ssm_out(zp, hf, hb, w_t, w_hf, w_hb, rt=512):
    n_rows = zp.shape[1]
    rt = min(rt, n_rows)
    half2 = 4 * SSM_STATE
    return pl.pallas_call(
        _ssm_out_kernel,
        grid=(SSM_PAIRS, n_rows // rt),
        in_specs=[pl.BlockSpec((None, rt, PAIR_WIDTH), lambda m, i: (m, i, 0)),
                  pl.BlockSpec((None, rt, half2), lambda m, i: (m, i, 0)),
                  pl.BlockSpec((None, rt, half2), lambda m, i: (m, i, 0)),
                  pl.BlockSpec((None, PAIR_WIDTH, PAIR_WIDTH), lambda m, i: (m, 0, 0)),
                  pl.BlockSpec((None, PAIR_WIDTH, half2), lambda m, i: (m, 0, 0)),
                  pl.BlockSpec((None, PAIR_WIDTH, half2), lambda m, i: (m, 0, 0))],
        out_specs=pl.BlockSpec((None, rt, PAIR_WIDTH), lambda m, i: (m, i, 0)),
        out_shape=jax.ShapeDtypeStruct((SSM_PAIRS, n_rows, PAIR_WIDTH), BF16),
        compiler_params=pltpu.CompilerParams(dimension_semantics=("arbitrary", "arbitrary")),
        name="ssm_out",
    )(zp, hf, hb, w_t, w_hf, w_hb)


def _pool_kernel(pu_ref, pg_ref, w_ref, scale_ref, o_ref, pad_ref, *, seq, rows):
    halo = POOL_HALO
    zeros = jnp.zeros((halo, POOL_WIDTH), F32)
    pad_ref[0:halo, :] = zeros
    pad_ref[seq + halo:seq + 2 * halo, :] = zeros
    pad_ref[halo:seq + halo, :] = pu_ref[...].astype(F32)

    lane = lax.broadcasted_iota(jnp.int32, (rows, POOL_WIDTH), 1)
    half_win = jnp.where(lane < POOL_GROUP, 1,
                         jnp.where(lane < 2 * POOL_GROUP, 2,
                                   jnp.where(lane < 3 * POOL_GROUP, 4, 8)))
    row = lax.broadcasted_iota(jnp.int32, (rows, POOL_WIDTH), 0)

    ext = rows + 2 * halo

    def shift(a, d):
        return pltpu.roll(a, (-d) % ext, axis=0)

    def body(k, _):
        base = pl.multiple_of(k * rows, rows)
        xs = pad_ref[pl.ds(base, ext), :]
        w2 = shift(xs, -1) + xs
        w4 = shift(w2, -1) + shift(w2, 1)
        w8 = shift(w4, -2) + shift(w4, 2)
        w16 = shift(w8, -4) + shift(w8, 4)
        centre = xs[halo:halo + rows]
        total = jnp.where(half_win == 1, w2[halo:halo + rows],
                          jnp.where(half_win == 2, w4[halo:halo + rows],
                                    jnp.where(half_win == 4, w8[halo:halo + rows],
                                              w16[halo:halo + rows])))
        t = row + base
        count = jnp.minimum(t + half_win, seq) - jnp.maximum(t - half_win, 0)
        diff = total / count.astype(F32) - centre
        y = jnp.dot(diff.astype(BF16), w_ref[...], preferred_element_type=F32)
        g = pg_ref[pl.ds(base, rows), :].astype(F32)
        o_ref[pl.ds(base, rows), :] = (y * scale_ref[...] * (g * jax.nn.sigmoid(g))).astype(BF16)
        return 0

    lax.fori_loop(0, seq // rows, body, 0)


def _pool(proj, w_bd, scale, bsz, seq, rows=256):
    kern = functools.partial(_pool_kernel, seq=seq, rows=rows)
    return pl.pallas_call(
        kern,
        grid=(bsz,),
        in_specs=[pl.BlockSpec((seq, POOL_WIDTH), lambda b: (b, COL_PU // POOL_WIDTH)),
                  pl.BlockSpec((seq, POOL_WIDTH), lambda b: (b, COL_PG // POOL_WIDTH)),
                  pl.BlockSpec((POOL_WIDTH, POOL_WIDTH), lambda b: (0, 0)),
                  pl.BlockSpec((1, POOL_WIDTH), lambda b: (0, 0))],
        out_specs=pl.BlockSpec((seq, POOL_WIDTH), lambda b: (b, 0)),
        out_shape=jax.ShapeDtypeStruct((bsz * seq, POOL_WIDTH), BF16),
        scratch_shapes=[pltpu.VMEM((seq + 2 * POOL_HALO, POOL_WIDTH), F32)],
        compiler_params=pltpu.CompilerParams(dimension_semantics=("arbitrary",)),
        name="pool",
    )(proj, proj, w_bd, scale)


def _attn_bias():
    slopes = np.exp2(-8.0 * np.arange(1, N_HEADS + 1, dtype=np.float32) / N_HEADS).astype(np.float32)
    qpos = np.arange(BLOCK)[:, None]
    kpos = np.arange(3 * BLOCK)[None, :] - BLOCK
    dist = np.abs(qpos - kpos)
    bias = -slopes[:, None, None] * dist.astype(np.float32)[None]
    bias = np.where(dist[None] <= WINDOW, bias * np.float32(LOG2_E), np.float32(NEG_BIG)).astype(np.float32)
    first, last = bias.copy(), bias.copy()
    first[:, :, :BLOCK] = NEG_BIG
    last[:, :, 2 * BLOCK:] = NEG_BIG
    return np.stack([first, bias, last])


def _attn_kernel(sink_ref, q_ref, kp_ref, kc_ref, kn_ref, vp_ref, vc_ref, vn_ref, ag_ref, bias_ref,
                 o_ref, s_ref, p_ref, *, n_blocks, qb):
    n = pl.program_id(1)
    nt = (((1,), (1,)), ((), ()))
    band = 3 * BLOCK
    keys = (qb + 2) * BLOCK
    lo = lax.broadcasted_iota(jnp.int32, (keys, KV_WIDTH), 1) < HEAD_DIM

    def per_head_halves(a_ref3):
        a = jnp.concatenate([r[...] for r in a_ref3], axis=0).astype(F32)
        swapped = pltpu.roll(a, HEAD_DIM, axis=1)
        return [[jnp.where(lo, a, 0.0).astype(BF16), jnp.where(lo, 0.0, swapped).astype(BF16)],
                [jnp.where(lo, swapped, 0.0).astype(BF16), jnp.where(lo, 0.0, a).astype(BF16)]]

    k_sel = per_head_halves((kp_ref, kc_ref, kn_ref))
    v_sel = per_head_halves((vp_ref, vc_ref, vn_ref))
    lo_q = lax.broadcasted_iota(jnp.int32, (BLOCK, 2 * HEAD_DIM), 1) < HEAD_DIM

    for j in range(qb):
        blk = n * qb + j
        variant = jnp.where(blk == 0, 0, jnp.where(blk == n_blocks - 1, 2, 1))
        q_rows = slice(j * BLOCK, (j + 1) * BLOCK)
        k_rows = slice(j * BLOCK, j * BLOCK + band)

        for h in range(N_HEADS):
            pair, odd = divmod(h, 2)
            qp = q_ref[q_rows, pair * 2 * HEAD_DIM:(pair + 1) * 2 * HEAD_DIM]
            s_ref[j, h] = (lax.dot_general(qp, k_sel[h // Q_PER_KV][odd][k_rows], nt,
                                           preferred_element_type=F32) + bias_ref[variant, h])

        inv_l = []
        for h in range(N_HEADS):
            s = s_ref[j, h]
            sink = sink_ref[h]
            m = jnp.maximum(jnp.max(s, axis=-1, keepdims=True), sink)
            p = jnp.exp2(s - m)
            inv_l.append(1.0 / (jnp.sum(p, axis=-1, keepdims=True) + jnp.exp2(sink - m)))
            p_ref[j, h] = p.astype(BF16)

        for pair in range(N_HEADS // 2):
            kv = (2 * pair) // Q_PER_KV
            o = (jnp.dot(p_ref[j, 2 * pair], v_sel[kv][0][k_rows], preferred_element_type=F32)
                 + jnp.dot(p_ref[j, 2 * pair + 1], v_sel[kv][1][k_rows], preferred_element_type=F32))
            o = o * jnp.where(lo_q, inv_l[2 * pair], inv_l[2 * pair + 1])
            sl = slice(pair * 2 * HEAD_DIM, (pair + 1) * 2 * HEAD_DIM)
            g = ag_ref[q_rows, sl].astype(F32)
            o_ref[q_rows, sl] = (o * (g * jax.nn.sigmoid(g))).astype(BF16)


def _attention(proj, sink, bias, bsz, seq, qb=ATTN_Q_BLOCKS):
    nb = seq // BLOCK
    assert nb >= 2 and nb % qb == 0
    steps = nb // qb
    kern = functools.partial(_attn_kernel, n_blocks=nb, qb=qb)
    kcol, vcol = COL_K // KV_WIDTH, COL_V // KV_WIDTH
    tq = qb * BLOCK

    def own(width, col):
        return pl.BlockSpec((tq, width), lambda b, n: (b * steps + n, col))

    def neighbour(col, prev):
        def index(b, n):
            blk = jnp.maximum(n * qb - 1, 0) if prev else jnp.minimum((n + 1) * qb, nb - 1)
            return (b * nb + blk, col)
        return pl.BlockSpec((BLOCK, KV_WIDTH), index)

    return pl.pallas_call(
        kern,
        grid=(bsz, steps),
        in_specs=[pl.BlockSpec(memory_space=pltpu.SMEM),
                  own(ATTN_WIDTH, COL_Q // ATTN_WIDTH),
                  neighbour(kcol, True), own(KV_WIDTH, kcol), neighbour(kcol, False),
                  neighbour(vcol, True), own(KV_WIDTH, vcol), neighbour(vcol, False),
                  own(ATTN_WIDTH, COL_AG // ATTN_WIDTH),
                  pl.BlockSpec((3, N_HEADS, BLOCK, 3 * BLOCK), lambda b, n: (0, 0, 0, 0))],
        out_specs=pl.BlockSpec((tq, ATTN_WIDTH), lambda b, n: (b * steps + n, 0)),
        out_shape=jax.ShapeDtypeStruct((bsz * seq, ATTN_WIDTH), BF16),
        scratch_shapes=[pltpu.VMEM((qb, N_HEADS, BLOCK, 3 * BLOCK), F32),
                        pltpu.VMEM((qb, N_HEADS, BLOCK, 3 * BLOCK), BF16)],
        compiler_params=pltpu.CompilerParams(dimension_semantics=("arbitrary", "arbitrary"),
                                             vmem_limit_bytes=48 * 1024 * 1024),
        name="attention",
    )(sink, proj, proj, proj, proj, proj, proj, proj, proj, bias)


def _out_kernel(x_ref, yp_ref, su_ref, sg_ref, ypool_ref, yattn_ref, d_ref, gw_ref, gb_ref,
                wo_ref, pg_ref, o_ref, ypre_ref):
    bsz, tb, _ = x_ref.shape
    tm = bsz * tb
    _from_chunk_rows(yp_ref, ypre_ref, bsz, tb)
    u = su_ref[...].reshape(tm, SSM_WIDTH).astype(F32)
    ypre = jnp.concatenate([ypre_ref[c] for c in range(SSM_WIDTH // LANES)], axis=1)
    y = jax.nn.gelu(ypre + d_ref[...] * u)
    z = jnp.dot(y.astype(BF16), gw_ref[...], preferred_element_type=F32) + gb_ref[...]
    sg = sg_ref[...].reshape(tm, SSM_WIDTH).astype(F32)
    y_ssm = z[:, :SSM_WIDTH] * jax.nn.sigmoid(z[:, SSM_WIDTH:]) * (sg * jax.nn.sigmoid(sg))
    acc = jnp.dot(y_ssm.astype(BF16), wo_ref[0:SSM_WIDTH, :], preferred_element_type=F32)
    acc = acc + jnp.dot(ypool_ref[...].reshape(tm, POOL_WIDTH), wo_ref[SSM_WIDTH:SSM_WIDTH + POOL_WIDTH, :],
                        preferred_element_type=F32)
    acc = acc + jnp.dot(yattn_ref[...].reshape(tm, ATTN_WIDTH), wo_ref[SSM_WIDTH + POOL_WIDTH:, :],
                        preferred_element_type=F32)
    ms = jnp.mean(acc * acc, axis=-1, keepdims=True)
    y_out = acc * lax.rsqrt(ms + RMS_EPS) * pg_ref[...]
    o_ref[...] = x_ref[...] + y_out.reshape(bsz, tb, D_MODEL)


def _out_proj(x, yp, proj, ypool, yattn, d, glu_w, glu_b, w_out, post_g, tb=TOKENS_PER_TILE):
    bsz, seq, _ = x.shape
    tm = bsz * tb
    tok = lambda c: (lambda i: (0, i, c))
    const = lambda i: (0, 0)
    return pl.pallas_call(
        _out_kernel,
        grid=(seq // tb,),
        in_specs=[pl.BlockSpec((bsz, tb, D_MODEL), tok(0)),
                  pl.BlockSpec((SSM_PAIRS, tm // SSM_CHUNK, PAIR_WIDTH), lambda i: (0, i, 0)),
                  pl.BlockSpec((bsz, tb, SSM_WIDTH), tok(COL_SU // SSM_WIDTH)),
                  pl.BlockSpec((bsz, tb, SSM_WIDTH), tok(COL_SG // SSM_WIDTH)),
                  pl.BlockSpec((bsz, tb, POOL_WIDTH), tok(0)),
                  pl.BlockSpec((bsz, tb, ATTN_WIDTH), tok(0)),
                  pl.BlockSpec((1, SSM_WIDTH), const),
                  pl.BlockSpec((SSM_WIDTH, 2 * SSM_WIDTH), const),
                  pl.BlockSpec((1, 2 * SSM_WIDTH), const),
                  pl.BlockSpec((D_MODEL, D_MODEL), const),
                  pl.BlockSpec((1, D_MODEL), const)],
        out_specs=pl.BlockSpec((bsz, tb, D_MODEL), tok(0)),
        out_shape=jax.ShapeDtypeStruct((bsz, seq, D_MODEL), F32),
        scratch_shapes=[pltpu.VMEM((SSM_WIDTH // LANES, tm, LANES), F32)],
        compiler_params=pltpu.CompilerParams(dimension_semantics=("arbitrary",),
                                             vmem_limit_bytes=48 * 1024 * 1024),
        name="out_proj",
    )(x, yp, proj, proj, ypool, yattn, d, glu_w, glu_b, w_out, post_g)


def kernel(x, pre_norm_g, w_in, ssm_a_re, ssm_a_im, ssm_log_dt, ssm_b_re, ssm_b_im, ssm_c_re, ssm_c_im,
           ssm_d, ssm_glu_w, ssm_glu_b, pool_w, pool_scale, attn_sink, w_out, post_norm_g):
    bsz, seq, _ = x.shape
    depth = w_in.shape[0]
    assert seq % BLOCK == 0 and seq % SSM_CHUNK == 0 and bsz == SUBLANES

    col_scale = np.ones((IN_WIDTH,), np.float32)
    col_scale[COL_Q:COL_Q + ATTN_WIDTH] = HEAD_DIM ** -0.5 * LOG2_E
    w_in_b = (_permute_in_cols(w_in) * col_scale).astype(BF16)
    w_out_b = w_out.astype(BF16)
    glu_w_b = ssm_glu_w.astype(BF16)
    eye = jnp.eye(len(POOL_WINDOWS), dtype=F32)
    pool_bd = jnp.einsum('lgcd,gh->lgchd', pool_w.astype(F32), eye).reshape(
        depth, POOL_WIDTH, POOL_WIDTH).astype(BF16)
    w_t, w_sf, w_sb, w_hf, w_hb, lam = _ssm_weights(
        ssm_a_re.astype(F32), ssm_a_im.astype(F32), ssm_log_dt.astype(F32), ssm_b_re.astype(F32),
        ssm_b_im.astype(F32), ssm_c_re.astype(F32), ssm_c_im.astype(F32))
    bias = jnp.asarray(_attn_bias())

    for l in range(depth):
        proj, zp = _inproj(x, pre_norm_g[l].reshape(1, D_MODEL), w_in_b[l])
        hf, hb = _ssm_scan(zp, w_sf[l], w_sb[l], lam[l])
        yp = _ssm_out(zp, hf, hb, w_t[l], w_hf[l], w_hb[l])
        proj2d = proj.reshape(bsz * seq, IN_WIDTH)
        ypool = _pool(proj2d, pool_bd[l], pool_scale[l].reshape(1, POOL_WIDTH), bsz, seq)
        yattn = _attention(proj2d, attn_sink[l].astype(F32) * LOG2_E, bias, bsz, seq)
        x = _out_proj(x, yp, proj, ypool.reshape(bsz, seq, POOL_WIDTH), yattn.reshape(bsz, seq, ATTN_WIDTH),
                      ssm_d[l].reshape(1, SSM_WIDTH), glu_w_b[l], ssm_glu_b[l].reshape(1, 2 * SSM_WIDTH),
                      w_out_b[l], post_norm_g[l].reshape(1, D_MODEL))
    return x
```

```python
import functools
import math

import numpy as np
import jax
import jax.numpy as jnp
from jax import lax
from jax.experimental import pallas as pl
from jax.experimental.pallas import tpu as pltpu

F32 = jnp.float32
BF16 = jnp.bfloat16
HIGHEST = lax.Precision.HIGHEST

D_MODEL = 1024
SSM_WIDTH = 256
POOL_WIDTH = 256
ATTN_WIDTH = 512
SSM_GROUP = 16
SSM_GROUPS = SSM_WIDTH // SSM_GROUP
SSM_PAIRS = SSM_GROUPS // 2
SSM_STATE = 64
POOL_WINDOWS = (2, 4, 8, 16)
POOL_GROUP = POOL_WIDTH // len(POOL_WINDOWS)
POOL_HALO = 8
HEAD_DIM = 64
N_HEADS = ATTN_WIDTH // HEAD_DIM
N_KV_HEADS = 2
KV_WIDTH = N_KV_HEADS * HEAD_DIM
Q_PER_KV = N_HEADS // N_KV_HEADS
ATTN_Q_BLOCKS = 8
WINDOW = 128
BLOCK = 128
RMS_EPS = 1e-6
NEG_BIG = -1e30
LOG2_E = math.log2(math.e)

SSM_CHUNK = 16
CHUNK_WIDTH = SSM_CHUNK * SSM_GROUP
PAIR_WIDTH = 2 * CHUNK_WIDTH
SUBLANES = 8
TOKENS_PER_TILE = 128
LANES = 128
PIECE = 2 * SSM_GROUP

COL_SU, COL_SG, COL_PU, COL_PG = 0, 256, 512, 768
COL_Q, COL_AG, COL_K, COL_V = 1024, 1536, 2048, 2176
IN_WIDTH = COL_V + KV_WIDTH


def _permute_in_cols(w):
    ref_k, ref_v, ref_ag = 1536, 1664, 1792
    return jnp.concatenate([w[..., :ref_k], w[..., ref_ag:ref_ag + ATTN_WIDTH],
                            w[..., ref_k:ref_k + KV_WIDTH], w[..., ref_v:ref_v + KV_WIDTH]], axis=-1)


def _lane_group_masks(rows):
    lane = lax.broadcasted_iota(jnp.int32, (rows, LANES), 1)
    return [(lane >= k * PIECE) & (lane < (k + 1) * PIECE) for k in range(LANES // PIECE)]


def _to_chunk_rows(su_ref, z_ref, bsz, tb):
    cpt = tb // SSM_CHUNK
    masks = _lane_group_masks(cpt * bsz)
    per_vreg = LANES // PIECE
    rows = [[jnp.concatenate([su_ref[col, pl.ds(c * SSM_CHUNK + i, bsz, stride=tb), :] for c in range(cpt)],
                             axis=0) for col in range(SSM_WIDTH // LANES)]
            for i in range(SSM_CHUNK)]
    for m in range(SSM_PAIRS):
        col, src_off = divmod(m * PIECE, LANES)
        for t in range(PAIR_WIDTH // LANES):
            acc = None
            for k in range(per_vreg):
                r = pltpu.roll(rows[t * per_vreg + k][col], (k * PIECE - src_off) % LANES, axis=1)
                acc = r if acc is None else jnp.where(masks[k], r, acc)
            z_ref[m, :, t * LANES:(t + 1) * LANES] = acc.astype(BF16)


def _from_chunk_rows(y_ref, out_ref, bsz, tb):
    cpt = tb // SSM_CHUNK
    masks = _lane_group_masks(cpt * bsz)
    per_vreg = LANES // PIECE
    for t in range(PAIR_WIDTH // LANES):
        src = [y_ref[m, :, t * LANES:(t + 1) * LANES].astype(F32) for m in range(SSM_PAIRS)]
        for k in range(per_vreg):
            j = t * per_vreg + k
            for col in range(SSM_WIDTH // LANES):
                acc = None
                for mm in range(per_vreg):
                    r = pltpu.roll(src[col * per_vreg + mm], (mm * PIECE - k * PIECE) % LANES, axis=1)
                    acc = r if acc is None else jnp.where(masks[mm], r, acc)
                for c in range(cpt):
                    out_ref[col, pl.ds(c * SSM_CHUNK + j, bsz, stride=tb), :] = acc[c * bsz:(c + 1) * bsz]


def _inproj_kernel(x_ref, g_ref, w_ref, o_ref, z_ref, h_ref, su_ref):
    bsz, tb, _ = x_ref.shape
    x = x_ref[...].reshape(bsz * tb, D_MODEL)
    ms = jnp.mean(x * x, axis=-1, keepdims=True)
    h_ref[...] = (x * lax.rsqrt(ms + RMS_EPS) * g_ref[...]).astype(BF16)
    n_tile = 256
    for j in range(IN_WIDTH // n_tile):
        sl = slice(j * n_tile, (j + 1) * n_tile)
        r = jnp.dot(h_ref[...], w_ref[:, sl], preferred_element_type=F32)
        o_ref[:, :, sl] = r.astype(BF16).reshape(bsz, tb, n_tile)
        if j * n_tile == COL_SU:
            for c in range(SSM_WIDTH // LANES):
                su_ref[c] = r[:, c * LANES:(c + 1) * LANES]
    _to_chunk_rows(su_ref, z_ref, bsz, tb)


def _inproj(x, g, w, layer, tb=TOKENS_PER_TILE):
    bsz, seq, _ = x.shape
    tm = bsz * tb
    rows_per_tile = tm // SSM_CHUNK
    return pl.pallas_call(
        _inproj_kernel,
        grid=(seq // tb,),
        in_specs=[pl.BlockSpec((bsz, tb, D_MODEL), lambda i: (0, i, 0)),
                  pl.BlockSpec((None, 1, D_MODEL), lambda i: (layer, 0, 0)),
                  pl.BlockSpec((None, D_MODEL, IN_WIDTH), lambda i: (layer, 0, 0))],
        out_specs=[pl.BlockSpec((bsz, tb, IN_WIDTH), lambda i: (0, i, 0)),
                   pl.BlockSpec((SSM_PAIRS, rows_per_tile, PAIR_WIDTH), lambda i: (0, i, 0))],
        out_shape=[jax.ShapeDtypeStruct((bsz, seq, IN_WIDTH), BF16),
                   jax.ShapeDtypeStruct((SSM_PAIRS, bsz * seq // SSM_CHUNK, PAIR_WIDTH), BF16)],
        scratch_shapes=[pltpu.VMEM((tm, D_MODEL), BF16),
                        pltpu.VMEM((SSM_WIDTH // LANES, tm, LANES), F32)],
        compiler_params=pltpu.CompilerParams(dimension_semantics=("arbitrary",),
                                             vmem_limit_bytes=48 * 1024 * 1024),
        name="inproj",
    )(x, g, w)


def _ssm_weights(a_re, a_im, log_dt, b_re, b_im, c_re, c_im):
    depth = a_re.shape[0]
    T = SSM_CHUNK
    dt = jnp.exp(log_dt)[..., None]
    ar, ai = a_re * dt, a_im * dt
    mag = jnp.exp(ar)
    lb_re, lb_im = mag * jnp.cos(ai), mag * jnp.sin(ai)
    den = a_re * a_re + a_im * a_im
    num_re = lb_re - 1.0
    coef_re = (num_re * a_re + lb_im * a_im) / den
    coef_im = (lb_im * a_re - num_re * a_im) / den
    bb_re = coef_re[..., None] * b_re - coef_im[..., None] * b_im
    bb_im = coef_re[..., None] * b_im + coef_im[..., None] * b_re
    tau = jnp.arange(T + 1, dtype=F32)[:, None, None]
    pw_mag = jnp.exp(tau * ar[:, :, None])
    pw_re = pw_mag * jnp.cos(tau * ai[:, :, None])
    pw_im = pw_mag * jnp.sin(tau * ai[:, :, None])
    half = 2 * SSM_STATE
    pwp_re = pw_re.reshape(depth, 2, T + 1, SSM_PAIRS, half)
    pwp_im = pw_im.reshape(depth, 2, T + 1, SSM_PAIRS, half)
    eye2 = jnp.eye(2, dtype=F32)

    def pair_blocks(a):
        r = a.shape[3]
        a = a.reshape(depth, 2, SSM_PAIRS, 2, r, 1, SSM_STATE)
        return (a * eye2[:, None, :, None]).reshape(depth, 2, SSM_PAIRS, 2 * r, half)

    bp_re = pair_blocks(jnp.transpose(bb_re, (0, 1, 2, 4, 3)))
    bp_im = pair_blocks(jnp.transpose(bb_im, (0, 1, 2, 4, 3)))
    cp_re = pair_blocks(c_re)
    cp_im = pair_blocks(c_im)

    def scaled(direction, expo, x_re, x_im):
        pr = jnp.transpose(pwp_re[:, direction][:, expo], (0, 2, 1, 3))[:, :, :, None]
        pi = jnp.transpose(pwp_im[:, direction][:, expo], (0, 2, 1, 3))[:, :, :, None]
        xr, xi = x_re[:, direction][:, :, None], x_im[:, direction][:, :, None]
        return pr * xr - pi * xi, pr * xi + pi * xr

    def rows_by_step(re, im):
        return jnp.concatenate([re, im], axis=-1).reshape(depth, SSM_PAIRS, PAIR_WIDTH, 2 * half)

    w_sf = rows_by_step(*scaled(0, jnp.arange(T - 1, -1, -1), bp_re, bp_im))
    w_sb = rows_by_step(*scaled(1, jnp.arange(T), bp_re, bp_im))

    def state_out(direction, expo):
        m_re, m_im = scaled(direction, expo, cp_re, cp_im)
        return rows_by_step(m_re, -m_im)

    w_hf = state_out(0, jnp.arange(1, T + 1))
    w_hb = state_out(1, jnp.arange(T, 0, -1))

    def lag_kernels(direction):
        s_re, s_im = scaled(direction, jnp.arange(T), bp_re, bp_im)
        return (jnp.einsum('dmtxk,dmyk->dmxty', s_re, cp_re[:, direction], precision=HIGHEST)
                - jnp.einsum('dmtxk,dmyk->dmxty', s_im, cp_im[:, direction], precision=HIGHEST))

    kf, kb = lag_kernels(0), lag_kernels(1)
    by_lag = jnp.concatenate([jnp.flip(kb[:, :, :, 1:], axis=3), kf[:, :, :, :1] + kb[:, :, :, :1],
                              kf[:, :, :, 1:]], axis=3).reshape(depth, SSM_PAIRS, PIECE, (2 * T - 1) * PIECE)
    w_t = jnp.stack([by_lag[..., (T - 1 - i) * PIECE:(T - 1 - i) * PIECE + PAIR_WIDTH] for i in range(T)],
                    axis=2).reshape(depth, SSM_PAIRS, PAIR_WIDTH, PAIR_WIDTH)

    def lam_rows(p):
        return jnp.broadcast_to(p[:, :, None], (depth, SSM_PAIRS, SUBLANES, half))

    lam = jnp.stack([lam_rows(pwp_re[:, 0, T]), lam_rows(pwp_im[:, 0, T]),
                     lam_rows(pwp_re[:, 1, T]), lam_rows(pwp_im[:, 1, T])], axis=1)
    return (w_t.astype(BF16), w_sf.astype(BF16), w_sb.astype(BF16),
            w_hf.astype(BF16), w_hb.astype(BF16), lam)


def _ssm_scan_kernel(zf_ref, zb_ref, wsf_ref, wsb_ref, lam_ref, hf_out, hb_out,
                     sfr, sfi, sbr, sbi, carry):
    rc = zf_ref.shape[1]
    half = 2 * SSM_STATE

    @pl.when(pl.program_id(0) == 0)
    def _():
        carry[...] = jnp.zeros_like(carry)

    for m in range(SSM_PAIRS):
        s = jnp.dot(zf_ref[m], wsf_ref[m], preferred_element_type=F32)
        sfr[m] = s[:, :half]
        sfi[m] = s[:, half:]
        s = jnp.dot(zb_ref[m], wsb_ref[m], preferred_element_type=F32)
        sbr[m] = s[:, :half]
        sbi[m] = s[:, half:]

    lfr, lfi, lbr, lbi = lam_ref[0], lam_ref[1], lam_ref[2], lam_ref[3]
    n_steps = rc // SUBLANES

    def body(k, c):
        hfr, hfi, hbr, hbi = c
        r = pl.multiple_of(k * SUBLANES, SUBLANES)
        hf_out[:, pl.ds(r, SUBLANES), :half] = hfr
        hf_out[:, pl.ds(r, SUBLANES), half:] = hfi
        nfr = lfr * hfr - lfi * hfi + sfr[:, pl.ds(r, SUBLANES), :]
        nfi = lfr * hfi + lfi * hfr + sfi[:, pl.ds(r, SUBLANES), :]
        rb = pl.multiple_of(rc - SUBLANES - k * SUBLANES, SUBLANES)
        hb_out[:, pl.ds(rb, SUBLANES), :half] = hbr
        hb_out[:, pl.ds(rb, SUBLANES), half:] = hbi
        nbr = lbr * hbr - lbi * hbi + sbr[:, pl.ds(rb, SUBLANES), :]
        nbi = lbr * hbi + lbi * hbr + sbi[:, pl.ds(rb, SUBLANES), :]
        return nfr, nfi, nbr, nbi

    c = lax.fori_loop(0, n_steps, body, (carry[0], carry[1], carry[2], carry[3]))
    carry[0], carry[1], carry[2], carry[3] = c


def _ssm_scan(zp, w_sf, w_sb, lam, layer, rc=256):
    n_rows = zp.shape[1]
    n_tiles = n_rows // rc
    half = 2 * SSM_STATE
    out = jax.ShapeDtypeStruct((SSM_PAIRS, n_rows, 2 * half), F32)
    s_scratch = pltpu.VMEM((SSM_PAIRS, rc, half), F32)
    return pl.pallas_call(
        _ssm_scan_kernel,
        grid=(n_tiles,),
        in_specs=[pl.BlockSpec((SSM_PAIRS, rc, PAIR_WIDTH), lambda i: (0, i, 0)),
                  pl.BlockSpec((SSM_PAIRS, rc, PAIR_WIDTH), lambda i: (0, n_tiles - 1 - i, 0)),
                  pl.BlockSpec((None, SSM_PAIRS, PAIR_WIDTH, 2 * half), lambda i: (layer, 0, 0, 0)),
                  pl.BlockSpec((None, SSM_PAIRS, PAIR_WIDTH, 2 * half), lambda i: (layer, 0, 0, 0)),
                  pl.BlockSpec((None, 4, SSM_PAIRS, SUBLANES, half), lambda i: (layer, 0, 0, 0, 0))],
        out_specs=[pl.BlockSpec((SSM_PAIRS, rc, 2 * half), lambda i: (0, i, 0)),
                   pl.BlockSpec((SSM_PAIRS, rc, 2 * half), lambda i: (0, n_tiles - 1 - i, 0))],
        out_shape=[out, out],
        scratch_shapes=[s_scratch, s_scratch, s_scratch, s_scratch,
                        pltpu.VMEM((4, SSM_PAIRS, SUBLANES, half), F32)],
        compiler_params=pltpu.CompilerParams(dimension_semantics=("arbitrary",),
                                             vmem_limit_bytes=48 * 1024 * 1024),
        name="ssm_scan",
    )(zp, zp, w_sf, w_sb, lam)


def _ssm_out_kernel(z_ref, hf_ref, hb_ref, wt_ref, whf_ref, whb_ref, y_ref):
    nt = (((1,), (1,)), ((), ()))
    y = jnp.dot(z_ref[...], wt_ref[...], preferred_element_type=F32)
    y = y + lax.dot_general(hf_ref[...].astype(BF16), whf_ref[...], nt, preferred_element_type=F32)
    y = y + lax.dot_general(hb_ref[...].astype(BF16), whb_ref[...], nt, preferred_element_type=F32)
    y_ref[...] = y.astype(BF16)


def _ssm_out(zp, hf, hb, w_t, w_hf, w_hb, layer, rt=512):
    n_rows = zp.shape[1]
    rt = min(rt, n_rows)
    half2 = 4 * SSM_STATE
    return pl.pallas_call(
        _ssm_out_kernel,
        grid=(SSM_PAIRS, n_rows // rt),
        in_specs=[pl.BlockSpec((None, rt, PAIR_WIDTH), lambda m, i: (m, i, 0)),
                  pl.BlockSpec((None, rt, half2), lambda m, i: (m, i, 0)),
                  pl.BlockSpec((None, rt, half2), lambda m, i: (m, i, 0)),
                  pl.BlockSpec((None, None, PAIR_WIDTH, PAIR_WIDTH), lambda m, i: (layer, m, 0, 0)),
                  pl.BlockSpec((None, None, PAIR_WIDTH, half2), lambda m, i: (layer, m, 0, 0)),
                  pl.BlockSpec((None, None, PAIR_WIDTH, half2), lambda m, i: (layer, m, 0, 0))],
        out_specs=pl.BlockSpec((None, rt, PAIR_WIDTH), lambda m, i: (m, i, 0)),
        out_shape=jax.ShapeDtypeStruct((SSM_PAIRS, n_rows, PAIR_WIDTH), BF16),
        compiler_params=pltpu.CompilerParams(dimension_semantics=("arbitrary", "arbitrary")),
        name="ssm_out",
    )(zp, hf, hb, w_t, w_hf, w_hb)


def _pool_kernel(pu_ref, pg_ref, w_ref, scale_ref, o_ref, pad_ref, *, seq, rows):
    halo = POOL_HALO
    zeros = jnp.zeros((halo, POOL_WIDTH), F32)
    pad_ref[0:halo, :] = zeros
    pad_ref[seq + halo:seq + 2 * halo, :] = zeros
    pad_ref[halo:seq + halo, :] = pu_ref[...].astype(F32)

    lane = lax.broadcasted_iota(jnp.int32, (rows, POOL_WIDTH), 1)
    half_win = jnp.where(lane < POOL_GROUP, 1,
                         jnp.where(lane < 2 * POOL_GROUP, 2,
                                   jnp.where(lane < 3 * POOL_GROUP, 4, 8)))
    row = lax.broadcasted_iota(jnp.int32, (rows, POOL_WIDTH), 0)

    ext = rows + 2 * halo

    def shift(a, d):
        return pltpu.roll(a, (-d) % ext, axis=0)

    def body(k, _):
        base = pl.multiple_of(k * rows, rows)
        xs = pad_ref[pl.ds(base, ext), :]
        w2 = shift(xs, -1) + xs
        w4 = shift(w2, -1) + shift(w2, 1)
        w8 = shift(w4, -2) + shift(w4, 2)
        w16 = shift(w8, -4) + shift(w8, 4)
        centre = xs[halo:halo + rows]
        total = jnp.where(half_win == 1, w2[halo:halo + rows],
                          jnp.where(half_win == 2, w4[halo:halo + rows],
                                    jnp.where(half_win == 4, w8[halo:halo + rows],
                                              w16[halo:halo + rows])))
        t = row + base
        count = jnp.minimum(t + half_win, seq) - jnp.maximum(t - half_win, 0)
        diff = total / count.astype(F32) - centre
        y = jnp.dot(diff.astype(BF16), w_ref[...], preferred_element_type=F32)
        g = pg_ref[pl.ds(base, rows), :].astype(F32)
        o_ref[pl.ds(base, rows), :] = (y * scale_ref[...] * (g * jax.nn.sigmoid(g))).astype(BF16)
        return 0

    lax.fori_loop(0, seq // rows, body, 0)


def _pool(proj, w_bd, scale, layer, bsz, seq, rows=256):
    kern = functools.partial(_pool_kernel, seq=seq, rows=rows)
    return pl.pallas_call(
        kern,
        grid=(bsz,),
        in_specs=[pl.BlockSpec((seq, POOL_WIDTH), lambda b: (b, COL_PU // POOL_WIDTH)),
                  pl.BlockSpec((seq, POOL_WIDTH), lambda b: (b, COL_PG // POOL_WIDTH)),
                  pl.BlockSpec((None, POOL_WIDTH, POOL_WIDTH), lambda b: (layer, 0, 0)),
                  pl.BlockSpec((None, 1, POOL_WIDTH), lambda b: (layer, 0, 0))],
        out_specs=pl.BlockSpec((seq, POOL_WIDTH), lambda b: (b, 0)),
        out_shape=jax.ShapeDtypeStruct((bsz * seq, POOL_WIDTH), BF16),
        scratch_shapes=[pltpu.VMEM((seq + 2 * POOL_HALO, POOL_WIDTH), F32)],
        compiler_params=pltpu.CompilerParams(dimension_semantics=("arbitrary",)),
        name="pool",
    )(proj, proj, w_bd, scale)


def _attn_bias():
    slopes = np.exp2(-8.0 * np.arange(1, N_HEADS + 1, dtype=np.float32) / N_HEADS).astype(np.float32)
    qpos = np.arange(BLOCK)[:, None]
    kpos = np.arange(3 * BLOCK)[None, :] - BLOCK
    dist = np.abs(qpos - kpos)
    bias = -slopes[:, None, None] * dist.astype(np.float32)[None]
    bias = np.where(dist[None] <= WINDOW, bias * np.float32(LOG2_E), np.float32(NEG_BIG)).astype(np.float32)
    first, last = bias.copy(), bias.copy()
    first[:, :, :BLOCK] = NEG_BIG
    last[:, :, 2 * BLOCK:] = NEG_BIG
    return np.stack([first, bias, last])


def _attn_kernel(sink_ref, q_ref, kp_ref, kc_ref, kn_ref, vp_ref, vc_ref, vn_ref, ag_ref, bias_ref,
                 o_ref, s_ref, p_ref, *, n_blocks, qb, layer):
    n = pl.program_id(1)
    nt = (((1,), (1,)), ((), ()))
    band = 3 * BLOCK
    keys = (qb + 2) * BLOCK
    lo = lax.broadcasted_iota(jnp.int32, (keys, KV_WIDTH), 1) < HEAD_DIM

    def per_head_halves(a_ref3):
        a = jnp.concatenate([r[...] for r in a_ref3], axis=0).astype(F32)
        swapped = pltpu.roll(a, HEAD_DIM, axis=1)
        return [[jnp.where(lo, a, 0.0).astype(BF16), jnp.where(lo, 0.0, swapped).astype(BF16)],
                [jnp.where(lo, swapped, 0.0).astype(BF16), jnp.where(lo, 0.0, a).astype(BF16)]]

    k_sel = per_head_halves((kp_ref, kc_ref, kn_ref))
    v_sel = per_head_halves((vp_ref, vc_ref, vn_ref))
    lo_q = lax.broadcasted_iota(jnp.int32, (BLOCK, 2 * HEAD_DIM), 1) < HEAD_DIM

    for j in range(qb):
        blk = n * qb + j
        variant = jnp.where(blk == 0, 0, jnp.where(blk == n_blocks - 1, 2, 1))
        q_rows = slice(j * BLOCK, (j + 1) * BLOCK)
        k_rows = slice(j * BLOCK, j * BLOCK + band)

        for h in range(N_HEADS):
            pair, odd = divmod(h, 2)
            qp = q_ref[q_rows, pair * 2 * HEAD_DIM:(pair + 1) * 2 * HEAD_DIM]
            s_ref[j, h] = (lax.dot_general(qp, k_sel[h // Q_PER_KV][odd][k_rows], nt,
                                           preferred_element_type=F32) + bias_ref[variant, h])

        inv_l = []
        for h in range(N_HEADS):
            s = s_ref[j, h]
            sink = sink_ref[layer, h]
            m = jnp.maximum(jnp.max(s, axis=-1, keepdims=True), sink)
            p = jnp.exp2(s - m)
            inv_l.append(1.0 / (jnp.sum(p, axis=-1, keepdims=True) + jnp.exp2(sink - m)))
            p_ref[j, h] = p.astype(BF16)

        for pair in range(N_HEADS // 2):
            kv = (2 * pair) // Q_PER_KV
            o = (jnp.dot(p_ref[j, 2 * pair], v_sel[kv][0][k_rows], preferred_element_type=F32)
                 + jnp.dot(p_ref[j, 2 * pair + 1], v_sel[kv][1][k_rows], preferred_element_type=F32))
            o = o * jnp.where(lo_q, inv_l[2 * pair], inv_l[2 * pair + 1])
            sl = slice(pair * 2 * HEAD_DIM, (pair + 1) * 2 * HEAD_DIM)
            g = ag_ref[q_rows, sl].astype(F32)
            o_ref[q_rows, sl] = (o * (g * jax.nn.sigmoid(g))).astype(BF16)


def _attention(proj, sink, bias, layer, bsz, seq, qb=ATTN_Q_BLOCKS):
    nb = seq // BLOCK
    assert nb >= 2 and nb % qb == 0
    steps = nb // qb
    kern = functools.partial(_attn_kernel, n_blocks=nb, qb=qb, layer=layer)
    kcol, vcol = COL_K // KV_WIDTH, COL_V // KV_WIDTH
    tq = qb * BLOCK

    def own(width, col):
        return pl.BlockSpec((tq, width), lambda b, n: (b * steps + n, col))

    def neighbour(col, prev):
        def index(b, n):
            blk = jnp.maximum(n * qb - 1, 0) if prev else jnp.minimum((n + 1) * qb, nb - 1)
            return (b * nb + blk, col)
        return pl.BlockSpec((BLOCK, KV_WIDTH), index)

    return pl.pallas_call(
        kern,
        grid=(bsz, steps),
        in_specs=[pl.BlockSpec(memory_space=pltpu.SMEM),
                  own(ATTN_WIDTH, COL_Q // ATTN_WIDTH),
                  neighbour(kcol, True), own(KV_WIDTH, kcol), neighbour(kcol, False),
                  neighbour(vcol, True), own(KV_WIDTH, vcol), neighbour(vcol, False),
                  own(ATTN_WIDTH, COL_AG // ATTN_WIDTH),
                  pl.BlockSpec((3, N_HEADS, BLOCK, 3 * BLOCK), lambda b, n: (0, 0, 0, 0))],
        out_specs=pl.BlockSpec((tq, ATTN_WIDTH), lambda b, n: (b * steps + n, 0)),
        out_shape=jax.ShapeDtypeStruct((bsz * seq, ATTN_WIDTH), BF16),
        scratch_shapes=[pltpu.VMEM((qb, N_HEADS, BLOCK, 3 * BLOCK), F32),
                        pltpu.VMEM((qb, N_HEADS, BLOCK, 3 * BLOCK), BF16)],
        compiler_params=pltpu.CompilerParams(dimension_semantics=("arbitrary", "arbitrary"),
                                             vmem_limit_bytes=48 * 1024 * 1024),
        name="attention",
    )(sink, proj, proj, proj, proj, proj, proj, proj, proj, bias)


def _out_kernel(x_ref, yp_ref, su_ref, sg_ref, ypool_ref, yattn_ref, d_ref, gw_ref, gb_ref,
                wo_ref, pg_ref, o_ref, ypre_ref):
    bsz, tb, _ = x_ref.shape
    tm = bsz * tb
    _from_chunk_rows(yp_ref, ypre_ref, bsz, tb)
    u = su_ref[...].reshape(tm, SSM_WIDTH).astype(F32)
    ypre = jnp.concatenate([ypre_ref[c] for c in range(SSM_WIDTH // LANES)], axis=1)
    y = jax.nn.gelu(ypre + d_ref[...] * u)
    z = jnp.dot(y.astype(BF16), gw_ref[...], preferred_element_type=F32) + gb_ref[...]
    sg = sg_ref[...].reshape(tm, SSM_WIDTH).astype(F32)
    y_ssm = z[:, :SSM_WIDTH] * jax.nn.sigmoid(z[:, SSM_WIDTH:]) * (sg * jax.nn.sigmoid(sg))
    acc = jnp.dot(y_ssm.astype(BF16), wo_ref[0:SSM_WIDTH, :], preferred_element_type=F32)
    acc = acc + jnp.dot(ypool_ref[...].reshape(tm, POOL_WIDTH), wo_ref[SSM_WIDTH:SSM_WIDTH + POOL_WIDTH, :],
                        preferred_element_type=F32)
    acc = acc + jnp.dot(yattn_ref[...].reshape(tm, ATTN_WIDTH), wo_ref[SSM_WIDTH + POOL_WIDTH:, :],
                        preferred_element_type=F32)
    ms = jnp.mean(acc * acc, axis=-1, keepdims=True)
    y_out = acc * lax.rsqrt(ms + RMS_EPS) * pg_ref[...]
    o_ref[...] = x_ref[...] + y_out.reshape(bsz, tb, D_MODEL)


def _out_proj(x, yp, proj, ypool, yattn, d, glu_w, glu_b, w_out, post_g, layer, tb=TOKENS_PER_TILE):
    bsz, seq, _ = x.shape
    tm = bsz * tb
    tok = lambda c: (lambda i: (0, i, c))
    const = lambda i: (layer, 0, 0)
    return pl.pallas_call(
        _out_kernel,
        grid=(seq // tb,),
        in_specs=[pl.BlockSpec((bsz, tb, D_MODEL), tok(0)),
                  pl.BlockSpec((SSM_PAIRS, tm // SSM_CHUNK, PAIR_WIDTH), lambda i: (0, i, 0)),
                  pl.BlockSpec((bsz, tb, SSM_WIDTH), tok(COL_SU // SSM_WIDTH)),
                  pl.BlockSpec((bsz, tb, SSM_WIDTH), tok(COL_SG // SSM_WIDTH)),
                  pl.BlockSpec((bsz, tb, POOL_WIDTH), tok(0)),
                  pl.BlockSpec((bsz, tb, ATTN_WIDTH), tok(0)),
                  pl.BlockSpec((None, 1, SSM_WIDTH), const),
                  pl.BlockSpec((None, SSM_WIDTH, 2 * SSM_WIDTH), const),
                  pl.BlockSpec((None, 1, 2 * SSM_WIDTH), const),
                  pl.BlockSpec((None, D_MODEL, D_MODEL), const),
                  pl.BlockSpec((None, 1, D_MODEL), const)],
        out_specs=pl.BlockSpec((bsz, tb, D_MODEL), tok(0)),
        out_shape=jax.ShapeDtypeStruct((bsz, seq, D_MODEL), F32),
        scratch_shapes=[pltpu.VMEM((SSM_WIDTH // LANES, tm, LANES), F32)],
        compiler_params=pltpu.CompilerParams(dimension_semantics=("arbitrary",),
                                             vmem_limit_bytes=48 * 1024 * 1024),
        name="out_proj",
    )(x, yp, proj, proj, ypool, yattn, d, glu_w, glu_b, w_out, post_g)


def kernel(x, pre_norm_g, w_in, ssm_a_re, ssm_a_im, ssm_log_dt, ssm_b_re, ssm_b_im, ssm_c_re, ssm_c_im,
           ssm_d, ssm_glu_w, ssm_glu_b, pool_w, pool_scale, attn_sink, w_out, post_norm_g):
    bsz, seq, _ = x.shape
    depth = w_in.shape[0]
    assert seq % BLOCK == 0 and seq % SSM_CHUNK == 0 and bsz == SUBLANES

    col_scale = np.ones((IN_WIDTH,), np.float32)
    col_scale[COL_Q:COL_Q + ATTN_WIDTH] = HEAD_DIM ** -0.5 * LOG2_E
    w_in_b = (_permute_in_cols(w_in) * col_scale).astype(BF16)
    w_out_b = w_out.astype(BF16)
    glu_w_b = ssm_glu_w.astype(BF16)
    eye = jnp.eye(len(POOL_WINDOWS), dtype=F32)
    pool_bd = jnp.einsum('lgcd,gh->lgchd', pool_w.astype(F32), eye).reshape(
        depth, POOL_WIDTH, POOL_WIDTH).astype(BF16)
    w_t, w_sf, w_sb, w_hf, w_hb, lam = _ssm_weights(
        ssm_a_re.astype(F32), ssm_a_im.astype(F32), ssm_log_dt.astype(F32), ssm_b_re.astype(F32),
        ssm_b_im.astype(F32), ssm_c_re.astype(F32), ssm_c_im.astype(F32))
    bias = jnp.asarray(_attn_bias())

    pre_g = pre_norm_g.astype(F32).reshape(depth, 1, D_MODEL)
    post_g = post_norm_g.astype(F32).reshape(depth, 1, D_MODEL)
    skip_d = ssm_d.astype(F32).reshape(depth, 1, SSM_WIDTH)
    glu_b = ssm_glu_b.astype(F32).reshape(depth, 1, 2 * SSM_WIDTH)
    pool_s = pool_scale.astype(F32).reshape(depth, 1, POOL_WIDTH)
    sink2 = attn_sink.astype(F32) * LOG2_E
    for l in range(depth):
        proj, zp = _inproj(x, pre_g, w_in_b, l)
        hf, hb = _ssm_scan(zp, w_sf, w_sb, lam, l)
        yp = _ssm_out(zp, hf, hb, w_t, w_hf, w_hb, l)
        proj2d = proj.reshape(bsz * seq, IN_WIDTH)
        ypool = _pool(proj2d, pool_bd, pool_s, l, bsz, seq)
        yattn = _attention(proj2d, sink2, bias, l, bsz, seq)
        x = _out_proj(x, yp, proj, ypool.reshape(bsz, seq, POOL_WIDTH), yattn.reshape(bsz, seq, ATTN_WIDTH),
                      skip_d, glu_w_b, glu_b, w_out_b, post_g, l)
    return x
```

```python
import functools
import math

import numpy as np
import jax
import jax.numpy as jnp
from jax import lax
from jax.experimental import pallas as pl
from jax.experimental.pallas import tpu as pltpu

F32 = jnp.float32
BF16 = jnp.bfloat16
HIGHEST = lax.Precision.HIGHEST

D_MODEL = 1024
SSM_WIDTH = 256
POOL_WIDTH = 256
ATTN_WIDTH = 512
SSM_GROUP = 16
SSM_GROUPS = SSM_WIDTH // SSM_GROUP
SSM_PAIRS = SSM_GROUPS // 2
SSM_STATE = 64
POOL_WINDOWS = (2, 4, 8, 16)
POOL_GROUP = POOL_WIDTH // len(POOL_WINDOWS)
POOL_HALO = 8
HEAD_DIM = 64
N_HEADS = ATTN_WIDTH // HEAD_DIM
N_KV_HEADS = 2
KV_WIDTH = N_KV_HEADS * HEAD_DIM
Q_PER_KV = N_HEADS // N_KV_HEADS
ATTN_Q_BLOCKS = 8
WINDOW = 128
BLOCK = 128
RMS_EPS = 1e-6
NEG_BIG = -1e30
LOG2_E = math.log2(math.e)

SSM_CHUNK = 16
CHUNK_WIDTH = SSM_CHUNK * SSM_GROUP
PAIR_WIDTH = 2 * CHUNK_WIDTH
SUBLANES = 8
TOKENS_PER_TILE = 128
LANES = 128
PIECE = 2 * SSM_GROUP

COL_SU, COL_SG, COL_PU, COL_PG = 0, 256, 512, 768
COL_Q, COL_AG, COL_K, COL_V = 1024, 1536, 2048, 2176
IN_WIDTH = COL_V + KV_WIDTH


def _permute_in_cols(w):
    ref_k, ref_v, ref_ag = 1536, 1664, 1792
    return jnp.concatenate([w[..., :ref_k], w[..., ref_ag:ref_ag + ATTN_WIDTH],
                            w[..., ref_k:ref_k + KV_WIDTH], w[..., ref_v:ref_v + KV_WIDTH]], axis=-1)


def _lane_group_masks(rows):
    lane = lax.broadcasted_iota(jnp.int32, (rows, LANES), 1)
    return [(lane >= k * PIECE) & (lane < (k + 1) * PIECE) for k in range(LANES // PIECE)]


def _to_chunk_rows(su_ref, z_ref, bsz, tb):
    cpt = tb // SSM_CHUNK
    masks = _lane_group_masks(cpt * bsz)
    per_vreg = LANES // PIECE
    rows = [[jnp.concatenate([su_ref[col, pl.ds(c * SSM_CHUNK + i, bsz, stride=tb), :] for c in range(cpt)],
                             axis=0) for col in range(SSM_WIDTH // LANES)]
            for i in range(SSM_CHUNK)]
    for m in range(SSM_PAIRS):
        col, src_off = divmod(m * PIECE, LANES)
        for t in range(PAIR_WIDTH // LANES):
            acc = None
            for k in range(per_vreg):
                r = pltpu.roll(rows[t * per_vreg + k][col], (k * PIECE - src_off) % LANES, axis=1)
                acc = r if acc is None else jnp.where(masks[k], r, acc)
            z_ref[m, :, t * LANES:(t + 1) * LANES] = acc.astype(BF16)


def _from_chunk_rows(y_ref, out_ref, bsz, tb):
    cpt = tb // SSM_CHUNK
    masks = _lane_group_masks(cpt * bsz)
    per_vreg = LANES // PIECE
    for t in range(PAIR_WIDTH // LANES):
        src = [y_ref[m, :, t * LANES:(t + 1) * LANES].astype(F32) for m in range(SSM_PAIRS)]
        for k in range(per_vreg):
            j = t * per_vreg + k
            for col in range(SSM_WIDTH // LANES):
                acc = None
                for mm in range(per_vreg):
                    r = pltpu.roll(src[col * per_vreg + mm], (mm * PIECE - k * PIECE) % LANES, axis=1)
                    acc = r if acc is None else jnp.where(masks[mm], r, acc)
                for c in range(cpt):
                    out_ref[col, pl.ds(c * SSM_CHUNK + j, bsz, stride=tb), :] = acc[c * bsz:(c + 1) * bsz]


def _inproj_kernel(x_ref, g_ref, w_ref, o_ref, z_ref, h_ref, su_ref):
    bsz, tb, _ = x_ref.shape
    x = x_ref[...].reshape(bsz * tb, D_MODEL)
    ms = jnp.mean(x * x, axis=-1, keepdims=True)
    h_ref[...] = (x * lax.rsqrt(ms + RMS_EPS) * g_ref[...]).astype(BF16)
    n_tile = 256
    for j in range(IN_WIDTH // n_tile):
        sl = slice(j * n_tile, (j + 1) * n_tile)
        r = jnp.dot(h_ref[...], w_ref[:, sl], preferred_element_type=F32)
        o_ref[:, :, sl] = r.astype(BF16).reshape(bsz, tb, n_tile)
        if j * n_tile == COL_SU:
            for c in range(SSM_WIDTH // LANES):
                su_ref[c] = r[:, c * LANES:(c + 1) * LANES]
    _to_chunk_rows(su_ref, z_ref, bsz, tb)


def _inproj(x, g, w, layer, tb=TOKENS_PER_TILE):
    bsz, seq, _ = x.shape
    tm = bsz * tb
    rows_per_tile = tm // SSM_CHUNK
    return pl.pallas_call(
        _inproj_kernel,
        grid=(seq // tb,),
        in_specs=[pl.BlockSpec((bsz, tb, D_MODEL), lambda i: (0, i, 0)),
                  pl.BlockSpec((None, 1, D_MODEL), lambda i: (layer, 0, 0)),
                  pl.BlockSpec((None, D_MODEL, IN_WIDTH), lambda i: (layer, 0, 0))],
        out_specs=[pl.BlockSpec((bsz, tb, IN_WIDTH), lambda i: (0, i, 0)),
                   pl.BlockSpec((SSM_PAIRS, rows_per_tile, PAIR_WIDTH), lambda i: (0, i, 0))],
        out_shape=[jax.ShapeDtypeStruct((bsz, seq, IN_WIDTH), BF16),
                   jax.ShapeDtypeStruct((SSM_PAIRS, bsz * seq // SSM_CHUNK, PAIR_WIDTH), BF16)],
        scratch_shapes=[pltpu.VMEM((tm, D_MODEL), BF16),
                        pltpu.VMEM((SSM_WIDTH // LANES, tm, LANES), F32)],
        compiler_params=pltpu.CompilerParams(dimension_semantics=("arbitrary",),
                                             vmem_limit_bytes=48 * 1024 * 1024),
        name="inproj",
    )(x, g, w)


def _ssm_weights(a_re, a_im, log_dt, b_re, b_im, c_re, c_im):
    depth = a_re.shape[0]
    T = SSM_CHUNK
    dt = jnp.exp(log_dt)[..., None]
    ar, ai = a_re * dt, a_im * dt
    mag = jnp.exp(ar)
    lb_re, lb_im = mag * jnp.cos(ai), mag * jnp.sin(ai)
    den = a_re * a_re + a_im * a_im
    num_re = lb_re - 1.0
    coef_re = (num_re * a_re + lb_im * a_im) / den
    coef_im = (lb_im * a_re - num_re * a_im) / den
    bb_re = coef_re[..., None] * b_re - coef_im[..., None] * b_im
    bb_im = coef_re[..., None] * b_im + coef_im[..., None] * b_re
    tau = jnp.arange(T + 1, dtype=F32)[:, None, None]
    pw_mag = jnp.exp(tau * ar[:, :, None])
    pw_re = pw_mag * jnp.cos(tau * ai[:, :, None])
    pw_im = pw_mag * jnp.sin(tau * ai[:, :, None])
    half = 2 * SSM_STATE
    pwp_re = pw_re.reshape(depth, 2, T + 1, SSM_PAIRS, half)
    pwp_im = pw_im.reshape(depth, 2, T + 1, SSM_PAIRS, half)
    eye2 = jnp.eye(2, dtype=F32)

    def pair_blocks(a):
        r = a.shape[3]
        a = a.reshape(depth, 2, SSM_PAIRS, 2, r, 1, SSM_STATE)
        return (a * eye2[:, None, :, None]).reshape(depth, 2, SSM_PAIRS, 2 * r, half)

    bp_re = pair_blocks(jnp.transpose(bb_re, (0, 1, 2, 4, 3)))
    bp_im = pair_blocks(jnp.transpose(bb_im, (0, 1, 2, 4, 3)))
    cp_re = pair_blocks(c_re)
    cp_im = pair_blocks(c_im)
    pwp_re, pwp_im, bp_re, bp_im, cp_re, cp_im = lax.optimization_barrier(
        (pwp_re, pwp_im, bp_re, bp_im, cp_re, cp_im))

    def scaled(direction, expo, x_re, x_im):
        pr = jnp.transpose(pwp_re[:, direction][:, expo], (0, 2, 1, 3))[:, :, :, None]
        pi = jnp.transpose(pwp_im[:, direction][:, expo], (0, 2, 1, 3))[:, :, :, None]
        xr, xi = x_re[:, direction][:, :, None], x_im[:, direction][:, :, None]
        return pr * xr - pi * xi, pr * xi + pi * xr

    def rows_by_step(re, im):
        return jnp.concatenate([re, im], axis=-1).reshape(depth, SSM_PAIRS, PAIR_WIDTH, 2 * half)

    w_sf = rows_by_step(*scaled(0, jnp.arange(T - 1, -1, -1), bp_re, bp_im))
    w_sb = rows_by_step(*scaled(1, jnp.arange(T), bp_re, bp_im))

    def state_out(direction, expo):
        m_re, m_im = scaled(direction, expo, cp_re, cp_im)
        return rows_by_step(m_re, -m_im)

    w_hf = state_out(0, jnp.arange(1, T + 1))
    w_hb = state_out(1, jnp.arange(T, 0, -1))

    def lag_kernels(direction):
        s_re, s_im = scaled(direction, jnp.arange(T), bp_re, bp_im)
        return (jnp.einsum('dmtxk,dmyk->dmxty', s_re, cp_re[:, direction], precision=HIGHEST)
                - jnp.einsum('dmtxk,dmyk->dmxty', s_im, cp_im[:, direction], precision=HIGHEST))

    kf, kb = lag_kernels(0), lag_kernels(1)
    by_lag = jnp.concatenate([jnp.flip(kb[:, :, :, 1:], axis=3), kf[:, :, :, :1] + kb[:, :, :, :1],
                              kf[:, :, :, 1:]], axis=3).reshape(depth, SSM_PAIRS, PIECE, (2 * T - 1) * PIECE)
    w_t = jnp.stack([by_lag[..., (T - 1 - i) * PIECE:(T - 1 - i) * PIECE + PAIR_WIDTH] for i in range(T)],
                    axis=2).reshape(depth, SSM_PAIRS, PAIR_WIDTH, PAIR_WIDTH)

    def lam_rows(p):
        return jnp.broadcast_to(p[:, :, None], (depth, SSM_PAIRS, SUBLANES, half))

    lam = jnp.stack([lam_rows(pwp_re[:, 0, T]), lam_rows(pwp_im[:, 0, T]),
                     lam_rows(pwp_re[:, 1, T]), lam_rows(pwp_im[:, 1, T])], axis=1)
    return (w_t.astype(BF16), w_sf.astype(BF16), w_sb.astype(BF16),
            w_hf.astype(BF16), w_hb.astype(BF16), lam)


def _ssm_scan_kernel(zf_ref, zb_ref, wsf_ref, wsb_ref, lam_ref, hf_out, hb_out,
                     sfr, sfi, sbr, sbi, carry):
    rc = zf_ref.shape[1]
    half = 2 * SSM_STATE

    @pl.when(pl.program_id(0) == 0)
    def _():
        carry[...] = jnp.zeros_like(carry)

    for m in range(SSM_PAIRS):
        s = jnp.dot(zf_ref[m], wsf_ref[m], preferred_element_type=F32)
        sfr[m] = s[:, :half]
        sfi[m] = s[:, half:]
        s = jnp.dot(zb_ref[m], wsb_ref[m], preferred_element_type=F32)
        sbr[m] = s[:, :half]
        sbi[m] = s[:, half:]

    lfr, lfi, lbr, lbi = lam_ref[0], lam_ref[1], lam_ref[2], lam_ref[3]
    n_steps = rc // SUBLANES

    def body(k, c):
        hfr, hfi, hbr, hbi = c
        r = pl.multiple_of(k * SUBLANES, SUBLANES)
        hf_out[:, pl.ds(r, SUBLANES), :half] = hfr
        hf_out[:, pl.ds(r, SUBLANES), half:] = hfi
        nfr = lfr * hfr - lfi * hfi + sfr[:, pl.ds(r, SUBLANES), :]
        nfi = lfr * hfi + lfi * hfr + sfi[:, pl.ds(r, SUBLANES), :]
        rb = pl.multiple_of(rc - SUBLANES - k * SUBLANES, SUBLANES)
        hb_out[:, pl.ds(rb, SUBLANES), :half] = hbr
        hb_out[:, pl.ds(rb, SUBLANES), half:] = hbi
        nbr = lbr * hbr - lbi * hbi + sbr[:, pl.ds(rb, SUBLANES), :]
        nbi = lbr * hbi + lbi * hbr + sbi[:, pl.ds(rb, SUBLANES), :]
        return nfr, nfi, nbr, nbi

    c = lax.fori_loop(0, n_steps, body, (carry[0], carry[1], carry[2], carry[3]))
    carry[0], carry[1], carry[2], carry[3] = c


def _ssm_scan(zp, w_sf, w_sb, lam, layer, rc=256):
    n_rows = zp.shape[1]
    n_tiles = n_rows // rc
    half = 2 * SSM_STATE
    out = jax.ShapeDtypeStruct((SSM_PAIRS, n_rows, 2 * half), F32)
    s_scratch = pltpu.VMEM((SSM_PAIRS, rc, half), F32)
    return pl.pallas_call(
        _ssm_scan_kernel,
        grid=(n_tiles,),
        in_specs=[pl.BlockSpec((SSM_PAIRS, rc, PAIR_WIDTH), lambda i: (0, i, 0)),
                  pl.BlockSpec((SSM_PAIRS, rc, PAIR_WIDTH), lambda i: (0, n_tiles - 1 - i, 0)),
                  pl.BlockSpec((None, SSM_PAIRS, PAIR_WIDTH, 2 * half), lambda i: (layer, 0, 0, 0)),
                  pl.BlockSpec((None, SSM_PAIRS, PAIR_WIDTH, 2 * half), lambda i: (layer, 0, 0, 0)),
                  pl.BlockSpec((None, 4, SSM_PAIRS, SUBLANES, half), lambda i: (layer, 0, 0, 0, 0))],
        out_specs=[pl.BlockSpec((SSM_PAIRS, rc, 2 * half), lambda i: (0, i, 0)),
                   pl.BlockSpec((SSM_PAIRS, rc, 2 * half), lambda i: (0, n_tiles - 1 - i, 0))],
        out_shape=[out, out],
        scratch_shapes=[s_scratch, s_scratch, s_scratch, s_scratch,
                        pltpu.VMEM((4, SSM_PAIRS, SUBLANES, half), F32)],
        compiler_params=pltpu.CompilerParams(dimension_semantics=("arbitrary",),
                                             vmem_limit_bytes=48 * 1024 * 1024),
        name="ssm_scan",
    )(zp, zp, w_sf, w_sb, lam)


def _ssm_out_kernel(z_ref, hf_ref, hb_ref, wt_ref, whf_ref, whb_ref, y_ref):
    nt = (((1,), (1,)), ((), ()))
    y = jnp.dot(z_ref[...], wt_ref[...], preferred_element_type=F32)
    y = y + lax.dot_general(hf_ref[...].astype(BF16), whf_ref[...], nt, preferred_element_type=F32)
    y = y + lax.dot_general(hb_ref[...].astype(BF16), whb_ref[...], nt, preferred_element_type=F32)
    y_ref[...] = y.astype(BF16)


def _ssm_out(zp, hf, hb, w_t, w_hf, w_hb, layer, rt=512):
    n_rows = zp.shape[1]
    rt = min(rt, n_rows)
    half2 = 4 * SSM_STATE
    return pl.pallas_call(
        _ssm_out_kernel,
        grid=(SSM_PAIRS, n_rows // rt),
        in_specs=[pl.BlockSpec((None, rt, PAIR_WIDTH), lambda m, i: (m, i, 0)),
                  pl.BlockSpec((None, rt, half2), lambda m, i: (m, i, 0)),
                  pl.BlockSpec((None, rt, half2), lambda m, i: (m, i, 0)),
                  pl.BlockSpec((None, None, PAIR_WIDTH, PAIR_WIDTH), lambda m, i: (layer, m, 0, 0)),
                  pl.BlockSpec((None, None, PAIR_WIDTH, half2), lambda m, i: (layer, m, 0, 0)),
                  pl.BlockSpec((None, None, PAIR_WIDTH, half2), lambda m, i: (layer, m, 0, 0))],
        out_specs=pl.BlockSpec((None, rt, PAIR_WIDTH), lambda m, i: (m, i, 0)),
        out_shape=jax.ShapeDtypeStruct((SSM_PAIRS, n_rows, PAIR_WIDTH), BF16),
        compiler_params=pltpu.CompilerParams(dimension_semantics=("arbitrary", "arbitrary")),
        name="ssm_out",
    )(zp, hf, hb, w_t, w_hf, w_hb)


def _pool_kernel(pu_ref, pg_ref, w_ref, scale_ref, o_ref, pad_ref, *, seq, rows):
    halo = POOL_HALO
    zeros = jnp.zeros((halo, POOL_WIDTH), F32)
    pad_ref[0:halo, :] = zeros
    pad_ref[seq + halo:seq + 2 * halo, :] = zeros
    pad_ref[halo:seq + halo, :] = pu_ref[...].astype(F32)

    lane = lax.broadcasted_iota(jnp.int32, (rows, POOL_WIDTH), 1)
    half_win = jnp.where(lane < POOL_GROUP, 1,
                         jnp.where(lane < 2 * POOL_GROUP, 2,
                                   jnp.where(lane < 3 * POOL_GROUP, 4, 8)))
    row = lax.broadcasted_iota(jnp.int32, (rows, POOL_WIDTH), 0)

    ext = rows + 2 * halo

    def shift(a, d):
        return pltpu.roll(a, (-d) % ext, axis=0)

    def body(k, _):
        base = pl.multiple_of(k * rows, rows)
        xs = pad_ref[pl.ds(base, ext), :]
        w2 = shift(xs, -1) + xs
        w4 = shift(w2, -1) + shift(w2, 1)
        w8 = shift(w4, -2) + shift(w4, 2)
        w16 = shift(w8, -4) + shift(w8, 4)
        centre = xs[halo:halo + rows]
        total = jnp.where(half_win == 1, w2[halo:halo + rows],
                          jnp.where(half_win == 2, w4[halo:halo + rows],
                                    jnp.where(half_win == 4, w8[halo:halo + rows],
                                              w16[halo:halo + rows])))
        t = row + base
        count = jnp.minimum(t + half_win, seq) - jnp.maximum(t - half_win, 0)
        diff = total / count.astype(F32) - centre
        y = jnp.dot(diff.astype(BF16), w_ref[...], preferred_element_type=F32)
        g = pg_ref[pl.ds(base, rows), :].astype(F32)
        o_ref[pl.ds(base, rows), :] = (y * scale_ref[...] * (g * jax.nn.sigmoid(g))).astype(BF16)
        return 0

    lax.fori_loop(0, seq // rows, body, 0)


def _pool(proj, w_bd, scale, layer, bsz, seq, rows=256):
    kern = functools.partial(_pool_kernel, seq=seq, rows=rows)
    return pl.pallas_call(
        kern,
        grid=(bsz,),
        in_specs=[pl.BlockSpec((seq, POOL_WIDTH), lambda b: (b, COL_PU // POOL_WIDTH)),
                  pl.BlockSpec((seq, POOL_WIDTH), lambda b: (b, COL_PG // POOL_WIDTH)),
                  pl.BlockSpec((None, POOL_WIDTH, POOL_WIDTH), lambda b: (layer, 0, 0)),
                  pl.BlockSpec((None, 1, POOL_WIDTH), lambda b: (layer, 0, 0))],
        out_specs=pl.BlockSpec((seq, POOL_WIDTH), lambda b: (b, 0)),
        out_shape=jax.ShapeDtypeStruct((bsz * seq, POOL_WIDTH), BF16),
        scratch_shapes=[pltpu.VMEM((seq + 2 * POOL_HALO, POOL_WIDTH), F32)],
        compiler_params=pltpu.CompilerParams(dimension_semantics=("arbitrary",)),
        name="pool",
    )(proj, proj, w_bd, scale)


def _attn_bias():
    slopes = np.exp2(-8.0 * np.arange(1, N_HEADS + 1, dtype=np.float32) / N_HEADS).astype(np.float32)
    qpos = np.arange(BLOCK)[:, None]
    kpos = np.arange(3 * BLOCK)[None, :] - BLOCK
    dist = np.abs(qpos - kpos)
    bias = -slopes[:, None, None] * dist.astype(np.float32)[None]
    bias = np.where(dist[None] <= WINDOW, bias * np.float32(LOG2_E), np.float32(NEG_BIG)).astype(np.float32)
    first, last = bias.copy(), bias.copy()
    first[:, :, :BLOCK] = NEG_BIG
    last[:, :, 2 * BLOCK:] = NEG_BIG
    return np.stack([first, bias, last])


def _attn_kernel(sink_ref, q_ref, kp_ref, kc_ref, kn_ref, vp_ref, vc_ref, vn_ref, ag_ref, bias_ref,
                 o_ref, s_ref, p_ref, *, n_blocks, qb, layer):
    n = pl.program_id(1)
    nt = (((1,), (1,)), ((), ()))
    band = 3 * BLOCK
    keys = (qb + 2) * BLOCK
    lo = lax.broadcasted_iota(jnp.int32, (keys, KV_WIDTH), 1) < HEAD_DIM

    def per_head_halves(a_ref3):
        a = jnp.concatenate([r[...] for r in a_ref3], axis=0).astype(F32)
        swapped = pltpu.roll(a, HEAD_DIM, axis=1)
        return [[jnp.where(lo, a, 0.0).astype(BF16), jnp.where(lo, 0.0, swapped).astype(BF16)],
                [jnp.where(lo, swapped, 0.0).astype(BF16), jnp.where(lo, 0.0, a).astype(BF16)]]

    k_sel = per_head_halves((kp_ref, kc_ref, kn_ref))
    v_sel = per_head_halves((vp_ref, vc_ref, vn_ref))
    lo_q = lax.broadcasted_iota(jnp.int32, (BLOCK, 2 * HEAD_DIM), 1) < HEAD_DIM

    for j in range(qb):
        blk = n * qb + j
        variant = jnp.where(blk == 0, 0, jnp.where(blk == n_blocks - 1, 2, 1))
        q_rows = slice(j * BLOCK, (j + 1) * BLOCK)
        k_rows = slice(j * BLOCK, j * BLOCK + band)

        for h in range(N_HEADS):
            pair, odd = divmod(h, 2)
            qp = q_ref[q_rows, pair * 2 * HEAD_DIM:(pair + 1) * 2 * HEAD_DIM]
            s_ref[j, h] = (lax.dot_general(qp, k_sel[h // Q_PER_KV][odd][k_rows], nt,
                                           preferred_element_type=F32) + bias_ref[variant, h])

        inv_l = []
        for h in range(N_HEADS):
            s = s_ref[j, h]
            sink = sink_ref[layer, h]
            m = jnp.maximum(jnp.max(s, axis=-1, keepdims=True), sink)
            p = jnp.exp2(s - m)
            inv_l.append(1.0 / (jnp.sum(p, axis=-1, keepdims=True) + jnp.exp2(sink - m)))
            p_ref[j, h] = p.astype(BF16)

        for pair in range(N_HEADS // 2):
            kv = (2 * pair) // Q_PER_KV
            o = (jnp.dot(p_ref[j, 2 * pair], v_sel[kv][0][k_rows], preferred_element_type=F32)
                 + jnp.dot(p_ref[j, 2 * pair + 1], v_sel[kv][1][k_rows], preferred_element_type=F32))
            o = o * jnp.where(lo_q, inv_l[2 * pair], inv_l[2 * pair + 1])
            sl = slice(pair * 2 * HEAD_DIM, (pair + 1) * 2 * HEAD_DIM)
            g = ag_ref[q_rows, sl].astype(F32)
            o_ref[q_rows, sl] = (o * (g * jax.nn.sigmoid(g))).astype(BF16)


def _attention(proj, sink, bias, layer, bsz, seq, qb=ATTN_Q_BLOCKS):
    nb = seq // BLOCK
    assert nb >= 2 and nb % qb == 0
    steps = nb // qb
    kern = functools.partial(_attn_kernel, n_blocks=nb, qb=qb, layer=layer)
    kcol, vcol = COL_K // KV_WIDTH, COL_V // KV_WIDTH
    tq = qb * BLOCK

    def own(width, col):
        return pl.BlockSpec((tq, width), lambda b, n: (b * steps + n, col))

    def neighbour(col, prev):
        def index(b, n):
            blk = jnp.maximum(n * qb - 1, 0) if prev else jnp.minimum((n + 1) * qb, nb - 1)
            return (b * nb + blk, col)
        return pl.BlockSpec((BLOCK, KV_WIDTH), index)

    return pl.pallas_call(
        kern,
        grid=(bsz, steps),
        in_specs=[pl.BlockSpec(memory_space=pltpu.SMEM),
                  own(ATTN_WIDTH, COL_Q // ATTN_WIDTH),
                  neighbour(kcol, True), own(KV_WIDTH, kcol), neighbour(kcol, False),
                  neighbour(vcol, True), own(KV_WIDTH, vcol), neighbour(vcol, False),
                  own(ATTN_WIDTH, COL_AG // ATTN_WIDTH),
                  pl.BlockSpec((3, N_HEADS, BLOCK, 3 * BLOCK), lambda b, n: (0, 0, 0, 0))],
        out_specs=pl.BlockSpec((tq, ATTN_WIDTH), lambda b, n: (b * steps + n, 0)),
        out_shape=jax.ShapeDtypeStruct((bsz * seq, ATTN_WIDTH), BF16),
        scratch_shapes=[pltpu.VMEM((qb, N_HEADS, BLOCK, 3 * BLOCK), F32),
                        pltpu.VMEM((qb, N_HEADS, BLOCK, 3 * BLOCK), BF16)],
        compiler_params=pltpu.CompilerParams(dimension_semantics=("arbitrary", "arbitrary"),
                                             vmem_limit_bytes=48 * 1024 * 1024),
        name="attention",
    )(sink, proj, proj, proj, proj, proj, proj, proj, proj, bias)


def _out_kernel(x_ref, yp_ref, su_ref, sg_ref, ypool_ref, yattn_ref, d_ref, gw_ref, gb_ref,
                wo_ref, pg_ref, o_ref, ypre_ref):
    bsz, tb, _ = x_ref.shape
    tm = bsz * tb
    _from_chunk_rows(yp_ref, ypre_ref, bsz, tb)
    u = su_ref[...].reshape(tm, SSM_WIDTH).astype(F32)
    ypre = jnp.concatenate([ypre_ref[c] for c in range(SSM_WIDTH // LANES)], axis=1)
    y = jax.nn.gelu(ypre + d_ref[...] * u)
    z = jnp.dot(y.astype(BF16), gw_ref[...], preferred_element_type=F32) + gb_ref[...]
    sg = sg_ref[...].reshape(tm, SSM_WIDTH).astype(F32)
    y_ssm = z[:, :SSM_WIDTH] * jax.nn.sigmoid(z[:, SSM_WIDTH:]) * (sg * jax.nn.sigmoid(sg))
    acc = jnp.dot(y_ssm.astype(BF16), wo_ref[0:SSM_WIDTH, :], preferred_element_type=F32)
    acc = acc + jnp.dot(ypool_ref[...].reshape(tm, POOL_WIDTH), wo_ref[SSM_WIDTH:SSM_WIDTH + POOL_WIDTH, :],
                        preferred_element_type=F32)
    acc = acc + jnp.dot(yattn_ref[...].reshape(tm, ATTN_WIDTH), wo_ref[SSM_WIDTH + POOL_WIDTH:, :],
                        preferred_element_type=F32)
    ms = jnp.mean(acc * acc, axis=-1, keepdims=True)
    y_out = acc * lax.rsqrt(ms + RMS_EPS) * pg_ref[...]
    o_ref[...] = x_ref[...] + y_out.reshape(bsz, tb, D_MODEL)


def _out_proj(x, yp, proj, ypool, yattn, d, glu_w, glu_b, w_out, post_g, layer, tb=TOKENS_PER_TILE):
    bsz, seq, _ = x.shape
    tm = bsz * tb
    tok = lambda c: (lambda i: (0, i, c))
    const = lambda i: (layer, 0, 0)
    return pl.pallas_call(
        _out_kernel,
        grid=(seq // tb,),
        in_specs=[pl.BlockSpec((bsz, tb, D_MODEL), tok(0)),
                  pl.BlockSpec((SSM_PAIRS, tm // SSM_CHUNK, PAIR_WIDTH), lambda i: (0, i, 0)),
                  pl.BlockSpec((bsz, tb, SSM_WIDTH), tok(COL_SU // SSM_WIDTH)),
                  pl.BlockSpec((bsz, tb, SSM_WIDTH), tok(COL_SG // SSM_WIDTH)),
                  pl.BlockSpec((bsz, tb, POOL_WIDTH), tok(0)),
                  pl.BlockSpec((bsz, tb, ATTN_WIDTH), tok(0)),
                  pl.BlockSpec((None, 1, SSM_WIDTH), const),
                  pl.BlockSpec((None, SSM_WIDTH, 2 * SSM_WIDTH), const),
                  pl.BlockSpec((None, 1, 2 * SSM_WIDTH), const),
                  pl.BlockSpec((None, D_MODEL, D_MODEL), const),
                  pl.BlockSpec((None, 1, D_MODEL), const)],
        out_specs=pl.BlockSpec((bsz, tb, D_MODEL), tok(0)),
        out_shape=jax.ShapeDtypeStruct((bsz, seq, D_MODEL), F32),
        scratch_shapes=[pltpu.VMEM((SSM_WIDTH // LANES, tm, LANES), F32)],
        compiler_params=pltpu.CompilerParams(dimension_semantics=("arbitrary",),
                                             vmem_limit_bytes=48 * 1024 * 1024),
        name="out_proj",
    )(x, yp, proj, proj, ypool, yattn, d, glu_w, glu_b, w_out, post_g)


def kernel(x, pre_norm_g, w_in, ssm_a_re, ssm_a_im, ssm_log_dt, ssm_b_re, ssm_b_im, ssm_c_re, ssm_c_im,
           ssm_d, ssm_glu_w, ssm_glu_b, pool_w, pool_scale, attn_sink, w_out, post_norm_g):
    bsz, seq, _ = x.shape
    depth = w_in.shape[0]
    assert seq % BLOCK == 0 and seq % SSM_CHUNK == 0 and bsz == SUBLANES

    col_scale = np.ones((IN_WIDTH,), np.float32)
    col_scale[COL_Q:COL_Q + ATTN_WIDTH] = HEAD_DIM ** -0.5 * LOG2_E
    w_in_b = (_permute_in_cols(w_in) * col_scale).astype(BF16)
    w_out_b = w_out.astype(BF16)
    glu_w_b = ssm_glu_w.astype(BF16)
    eye = jnp.eye(len(POOL_WINDOWS), dtype=F32)
    pool_bd = jnp.einsum('lgcd,gh->lgchd', pool_w.astype(F32), eye).reshape(
        depth, POOL_WIDTH, POOL_WIDTH).astype(BF16)
    w_t, w_sf, w_sb, w_hf, w_hb, lam = _ssm_weights(
        ssm_a_re.astype(F32), ssm_a_im.astype(F32), ssm_log_dt.astype(F32), ssm_b_re.astype(F32),
        ssm_b_im.astype(F32), ssm_c_re.astype(F32), ssm_c_im.astype(F32))
    bias = jnp.asarray(_attn_bias())

    pre_g = pre_norm_g.astype(F32).reshape(depth, 1, D_MODEL)
    post_g = post_norm_g.astype(F32).reshape(depth, 1, D_MODEL)
    skip_d = ssm_d.astype(F32).reshape(depth, 1, SSM_WIDTH)
    glu_b = ssm_glu_b.astype(F32).reshape(depth, 1, 2 * SSM_WIDTH)
    pool_s = pool_scale.astype(F32).reshape(depth, 1, POOL_WIDTH)
    sink2 = attn_sink.astype(F32) * LOG2_E
    for l in range(depth):
        proj, zp = _inproj(x, pre_g, w_in_b, l)
        hf, hb = _ssm_scan(zp, w_sf, w_sb, lam, l)
        yp = _ssm_out(zp, hf, hb, w_t, w_hf, w_hb, l)
        proj2d = proj.reshape(bsz * seq, IN_WIDTH)
        ypool = _pool(proj2d, pool_bd, pool_s, l, bsz, seq)
        yattn = _attention(proj2d, sink2, bias, l, bsz, seq)
        x = _out_proj(x, yp, proj, ypool.reshape(bsz, seq, POOL_WIDTH), yattn.reshape(bsz, seq, ATTN_WIDTH),
                      skip_d, glu_w_b, glu_b, w_out_b, post_g, l)
    return x
```
